```python
import math
import jax, jax.numpy as jnp
from jax import lax
import numpy as np

D_MODEL = 2048
BATCH = 8
SEQ = 4096
DEPTH = 4

GRID_W = 64
CTX_LEN = 256
EPS = 1e-6
ROPE_BASE = 10000.0
CHUNK = 64
Q_BLOCK = 128

RET_HEADS = 4
RET_HD = 128
D_RET = RET_HEADS * RET_HD
GDN_HEADS = 4
GDN_HD = 128
D_GDN = GDN_HEADS * GDN_HD
GDN_CONV = 5
MLA_HEADS = 8
MLA_NOPE = 128
MLA_ROPE = 64
MLA_VD = 128
MLA_Q_RANK = 768
MLA_KV_RANK = 512
D_MLA = MLA_HEADS * MLA_VD
MLA_SCALE = (MLA_NOPE + MLA_ROPE) ** -0.5

D_MIX = D_RET + D_GDN + D_MLA

IN_SPLITS = (D_RET, D_RET, D_RET, D_RET,
             D_GDN, D_GDN, D_GDN, D_GDN, 2 * GDN_HEADS, 2 * GDN_HEADS,
             MLA_Q_RANK, MLA_KV_RANK, MLA_ROPE, D_MLA)
D_IN = sum(IN_SPLITS)

kernel_name = 'hybrid_ret_gdn_mla_prefix_dit'


def rmsnorm(x, w):
    xf = x.astype(jnp.float32)
    y = xf * lax.rsqrt(jnp.mean(xf * xf, axis=-1, keepdims=True) + EPS)
    return (y * w.astype(jnp.float32)).astype(x.dtype)


def l2norm(x):
    xf = x.astype(jnp.float32)
    return (xf * lax.rsqrt(jnp.sum(xf * xf, axis=-1, keepdims=True) + EPS)).astype(x.dtype)


def split_cols(z, sizes):
    return jnp.split(z, np.cumsum(sizes)[:-1].tolist(), axis=-1)


def heads(t, n_heads):
    return t.reshape(t.shape[:-1] + (n_heads, -1))


def to_bhld(t):
    return jnp.transpose(t, (0, 2, 1, 3))


def merge_heads(o):
    B, H, L, d = o.shape
    return jnp.transpose(o, (0, 2, 1, 3)).reshape(B, L, H * d)


def flip_seq(t):
    return None if t is None else jnp.flip(t, axis=2)


def axial_rope_tables(L, dim):
    rows = L // GRID_W
    row = jnp.repeat(jnp.arange(rows), GRID_W).astype(jnp.float32)
    col = jnp.tile(jnp.arange(GRID_W), rows).astype(jnp.float32)
    m = dim // 2
    inv = ROPE_BASE ** (-jnp.arange(m // 2, dtype=jnp.float32) / (m // 2))
    ar = row[:, None] * inv
    ac = col[:, None] * inv
    return (jnp.cos(ar), jnp.sin(ar), jnp.cos(ac), jnp.sin(ac))


def rope_half(x, cos, sin):
    x1, x2 = jnp.split(x, 2, axis=-1)
    c = cos[:, None, :].astype(x.dtype)
    s = sin[:, None, :].astype(x.dtype)
    return jnp.concatenate([x1 * c - x2 * s, x1 * s + x2 * c], axis=-1)


def axial_rope(x, rope):
    cos_r, sin_r, cos_c, sin_c = rope
    xr, xc = jnp.split(x, 2, axis=-1)
    return jnp.concatenate([rope_half(xr, cos_r, sin_r), rope_half(xc, cos_c, sin_c)], axis=-1)


def centred_dwconv(x, w):
    K = w.shape[0]
    return lax.conv_general_dilated(
        x, w[:, None, :].astype(x.dtype), window_strides=(1,), padding=[(K // 2, K // 2)],
        dimension_numbers=('NWC', 'WIO', 'NWC'), feature_group_count=x.shape[-1])


def chunk_scan(q, k, v, g, beta, s0):
    out_dtype = v.dtype
    f32 = jnp.float32
    q, k, v, g = (t.astype(f32) for t in (q, k, v, g))
    B, H, L, dk = q.shape
    dv = v.shape[-1]
    n = L // CHUNK
    rs = lambda t: t.reshape((B, H, n, CHUNK) + t.shape[3:])
    q, k, v, g = rs(q), rs(k), rs(v), rs(g)
    gc = jnp.cumsum(g, axis=-1)
    incl = jnp.tril(jnp.ones((CHUNK, CHUNK), bool))
    decay = jnp.exp(jnp.where(incl, gc[..., :, None] - gc[..., None, :], -jnp.inf))
    if beta is None:
        u, w = v, None
    else:
        beta = rs(beta.astype(f32))
        kb = k * beta[..., None]
        strict = jnp.tril(jnp.ones((CHUNK, CHUNK), bool), -1)
        a = jnp.where(strict, jnp.einsum('bhncd,bhnsd->bhncs', kb, k) * decay, 0.0)
        rhs = jnp.concatenate([v * beta[..., None], kb * jnp.exp(gc)[..., None]], axis=-1)
        sol = lax.linalg.triangular_solve(a + jnp.eye(CHUNK, dtype=f32), rhs, left_side=True,
                                          lower=True, unit_diagonal=True)
        u, w = sol[..., :dv], sol[..., dv:]
    att = jnp.einsum('bhncd,bhnsd->bhncs', q, k) * decay
    mv = lambda t: None if t is None else jnp.moveaxis(t, 2, 0)

    def step(s, xs):
        q_i, k_i, u_i, w_i, gc_i, att_i = xs
        v_new = u_i if w_i is None else u_i - jnp.einsum('bhcd,bhde->bhce', w_i, s)
        o_i = (jnp.einsum('bhcd,bhde->bhce', q_i * jnp.exp(gc_i)[..., None], s)
               + jnp.einsum('bhcs,bhse->bhce', att_i, v_new))
        g_last = gc_i[..., -1]
        k_dec = k_i * jnp.exp(g_last[..., None] - gc_i)[..., None]
        s = s * jnp.exp(g_last)[..., None, None] + jnp.einsum('bhcd,bhce->bhde', k_dec, v_new)
        return s, o_i

    s, o = lax.scan(step, s0.astype(f32), (mv(q), mv(k), mv(u), mv(w), mv(gc), mv(att)))
    o = jnp.moveaxis(o, 0, 2).reshape(B, H, L, dv)
    return o.astype(out_dtype), s


def bidir_scan(ctx_qkv, lat_qkv, g_ctx, g_lat, beta_ctx, beta_lat):
    qc, kc, vc = ctx_qkv
    B, H, _, dk = qc.shape
    s0 = jnp.zeros((B, H, dk, vc.shape[-1]), jnp.float32)
    outs_c, outs_l = [], []
    for d in range(2):
        fl = (lambda t: t) if d == 0 else flip_seq
        oc, s_ctx = chunk_scan(*(fl(t) for t in ctx_qkv), fl(g_ctx[d]), fl(beta_ctx[d]), s0)
        ol, _ = chunk_scan(*(fl(t) for t in lat_qkv), fl(g_lat[d]), fl(beta_lat[d]), s_ctx)
        outs_c.append(fl(oc))
        outs_l.append(fl(ol))
    return outs_c[0] + outs_c[1], outs_l[0] + outs_l[1]


def retention_mixer(zc, zl, log_decay, norm_w, rope, need_ctx):
    def prep(q, k, v, rot):
        q, k, v = (heads(t, RET_HEADS) for t in (q, k, v))
        if rot is not None:
            q, k = axial_rope(q, rot), axial_rope(k, rot)
        return to_bhld(q), to_bhld(k * RET_HD ** -0.5), to_bhld(v)

    ctx_qkv = prep(zc[0], zc[1], zc[2], None)
    lat_qkv = prep(zl[0], zl[1], zl[2], rope)
    B = lat_qkv[0].shape[0]
    decays = lambda n: tuple(jnp.broadcast_to(log_decay[d][None, :, None], (B, RET_HEADS, n))
                             for d in range(2))
    o_c, o_l = bidir_scan(ctx_qkv, lat_qkv, decays(ctx_qkv[0].shape[2]), decays(lat_qkv[0].shape[2]),
                          (None, None), (None, None))
    out = lambda o, gate: jax.nn.silu(gate) * merge_heads(rmsnorm(o, norm_w))
    return (out(o_c, zc[3]) if need_ctx else None), out(o_l, zl[3])


def gated_deltanet_mixer(zc, zl, conv_w, a_log, dt_bias, norm_w, need_ctx):
    def prep(q, k, v, a, b):
        qkv = jax.nn.silu(centred_dwconv(jnp.concatenate([q, k, v], axis=-1), conv_w))
        q, k, v = (heads(t, GDN_HEADS) for t in jnp.split(qkv, 3, axis=-1))
        q = l2norm(q) * GDN_HD ** -0.5
        k = l2norm(k)
        Bn, n = a.shape[:2]
        a = a.astype(jnp.float32).reshape(Bn, n, 2, GDN_HEADS)
        b = b.astype(jnp.float32).reshape(Bn, n, 2, GDN_HEADS)
        g = -jnp.exp(a_log.astype(jnp.float32)) * jax.nn.softplus(a + dt_bias.astype(jnp.float32))
        beta = jax.nn.sigmoid(b)
        per_dir = lambda t: tuple(jnp.transpose(t[:, :, d], (0, 2, 1)) for d in range(2))
        return (to_bhld(q), to_bhld(k), to_bhld(v)), per_dir(g), per_dir(beta)

    ctx_qkv, g_c, b_c = prep(zc[0], zc[1], zc[2], zc[4], zc[5])
    lat_qkv, g_l, b_l = prep(zl[0], zl[1], zl[2], zl[4], zl[5])
    o_c, o_l = bidir_scan(ctx_qkv, lat_qkv, g_c, g_l, b_c, b_l)
    out = lambda o, gate: jax.nn.silu(gate) * merge_heads(rmsnorm(o, norm_w))
    return (out(o_c, zc[3]) if need_ctx else None), out(o_l, zl[3])


def sdpa(q, k, v):
    s = jnp.einsum('bhqd,bhkd->bhqk', q, k).astype(jnp.float32) * MLA_SCALE
    p = jax.nn.softmax(s, axis=-1).astype(v.dtype)
    return jnp.einsum('bhqk,bhkd->bhqd', p, v)


def blocked_attention(q, k, v):
    B, H, L, d = q.shape
    nb = L // Q_BLOCK
    qb = jnp.moveaxis(q.reshape(B, H, nb, Q_BLOCK, d), 2, 0)
    o = lax.map(lambda qi: sdpa(qi, k, v), qb)
    return jnp.moveaxis(o, 0, 2).reshape(B, H, L, v.shape[-1])


def mla_mixer(zc, zl, q_norm_w, w_qb, kv_norm_w, w_kvb, rope, need_ctx):
    def prep(cq, ckv, kr, rot):
        q = heads(rmsnorm(cq, q_norm_w) @ w_qb, MLA_HEADS)
        kv = heads(rmsnorm(ckv, kv_norm_w) @ w_kvb, MLA_HEADS)
        q_nope, q_rope = q[..., :MLA_NOPE], q[..., MLA_NOPE:]
        k_nope, v = kv[..., :MLA_NOPE], kv[..., MLA_NOPE:]
        kr = kr[:, :, None, :]
        if rot is not None:
            q_rope, kr = axial_rope(q_rope, rot), axial_rope(kr, rot)
        k = jnp.concatenate([k_nope, jnp.broadcast_to(kr, k_nope.shape[:-1] + (MLA_ROPE,))], axis=-1)
        q = jnp.concatenate([q_nope, q_rope], axis=-1)
        return to_bhld(q), to_bhld(k), to_bhld(v)

    qc, kc, vc = prep(zc[0], zc[1], zc[2], None)
    ql, kl, vl = prep(zl[0], zl[1], zl[2], rope)
    k_all = jnp.concatenate([kc, kl], axis=2)
    v_all = jnp.concatenate([vc, vl], axis=2)
    o_l = jax.nn.silu(zl[3]) * merge_heads(blocked_attention(ql, k_all, v_all))
    o_c = jax.nn.silu(zc[3]) * merge_heads(sdpa(qc, kc, vc)) if need_ctx else None
    return o_c, o_l


def setup_inputs(seed: int = 0) -> dict:
    key = jax.random.key(seed)
    ks = jax.random.split(key, 20)
    f32 = jnp.float32

    def normal(k, shape, scale):
        return jax.random.normal(k, shape, f32) * scale

    def gain(k, shape):
        return 1.0 + 0.02 * jax.random.normal(k, shape, f32)

    base_decay = jnp.log(1.0 - 2.0 ** (-5.0 - jnp.arange(RET_HEADS, dtype=f32)))
    ret_log_decay = base_decay * jnp.exp(0.1 * jax.random.normal(ks[8], (DEPTH, 2, RET_HEADS), f32))
    dt = jnp.exp(jax.random.uniform(ks[10], (DEPTH, 2, GDN_HEADS), f32, math.log(1e-3), math.log(1e-1)))
    gdn_dt_bias = dt + jnp.log(-jnp.expm1(-dt))
    gdn_a_log = jnp.log(jax.random.uniform(ks[11], (DEPTH, 2, GDN_HEADS), f32, 1.0, 16.0))
    return {
        'x': normal(ks[0], (BATCH, SEQ, D_MODEL), 1.0),
        'c': normal(ks[1], (BATCH, D_MODEL), 1.0),
        'ctx': normal(ks[2], (BATCH, CTX_LEN, D_MODEL), 1.0),
        'c_ctx': normal(ks[3], (D_MODEL,), 1.0),
        'ada_w': normal(ks[4], (DEPTH, D_MODEL, 3 * D_MODEL), 0.5 * D_MODEL ** -0.5),
        'ada_b': normal(ks[5], (DEPTH, 3 * D_MODEL), 0.02),
        'norm_w': gain(ks[6], (DEPTH, D_MODEL)),
        'w_in': normal(ks[7], (DEPTH, D_MODEL, D_IN), D_MODEL ** -0.5),
        'ret_log_decay': ret_log_decay,
        'ret_norm_w': gain(ks[9], (DEPTH, RET_HD)),
        'gdn_conv_w': normal(ks[12], (DEPTH, GDN_CONV, 3 * D_GDN), GDN_CONV ** -0.5),
        'gdn_a_log': gdn_a_log,
        'gdn_dt_bias': gdn_dt_bias,
        'gdn_norm_w': gain(ks[13], (DEPTH, GDN_HD)),
        'mla_q_norm_w': gain(ks[14], (DEPTH, MLA_Q_RANK)),
        'mla_w_qb': normal(ks[15], (DEPTH, MLA_Q_RANK, MLA_HEADS * (MLA_NOPE + MLA_ROPE)), MLA_Q_RANK ** -0.5),
        'mla_kv_norm_w': gain(ks[16], (DEPTH, MLA_KV_RANK)),
        'mla_w_kvb': normal(ks[17], (DEPTH, MLA_KV_RANK, MLA_HEADS * (MLA_NOPE + MLA_VD)), MLA_KV_RANK ** -0.5),
        'w_out': normal(ks[18], (DEPTH, D_MIX, D_MODEL), D_MIX ** -0.5),
        'final_norm_w': gain(ks[19], (D_MODEL,)),
    }


def reference(x, c, ctx, c_ctx, ada_w, ada_b, norm_w, w_in, ret_log_decay, ret_norm_w,
              gdn_conv_w, gdn_a_log, gdn_dt_bias, gdn_norm_w, mla_q_norm_w, mla_w_qb,
              mla_kv_norm_w, mla_w_kvb, w_out, final_norm_w):
    L = x.shape[1]
    rope_ret = axial_rope_tables(L, RET_HD)
    rope_mla = axial_rope_tables(L, MLA_ROPE)
    xc = ctx
    for i in range(DEPTH):
        last = i == DEPTH - 1
        shift, scale, gate = jnp.split(jax.nn.silu(c) @ ada_w[i] + ada_b[i], 3, axis=-1)
        shift_c, scale_c, gate_c = jnp.split(jax.nn.silu(c_ctx) @ ada_w[i] + ada_b[i], 3, axis=-1)
        h = rmsnorm(x, norm_w[i]) * (1.0 + scale[:, None]) + shift[:, None]
        hc = rmsnorm(xc, norm_w[i]) * (1.0 + scale_c) + shift_c
        zl = split_cols(h @ w_in[i], IN_SPLITS)
        zc = split_cols(hc @ w_in[i], IN_SPLITS)
        ret_c, ret_l = retention_mixer(zc[0:4], zl[0:4], ret_log_decay[i], ret_norm_w[i], rope_ret, not last)
        gdn_c, gdn_l = gated_deltanet_mixer(zc[4:10], zl[4:10], gdn_conv_w[i], gdn_a_log[i],
                                            gdn_dt_bias[i], gdn_norm_w[i], not last)
        mla_c, mla_l = mla_mixer(zc[10:14], zl[10:14], mla_q_norm_w[i], mla_w_qb[i], mla_kv_norm_w[i],
                                 mla_w_kvb[i], rope_mla, not last)
        x = x + gate[:, None] * (jnp.concatenate([ret_l, gdn_l, mla_l], axis=-1) @ w_out[i])
        if not last:
            xc = xc + gate_c * (jnp.concatenate([ret_c, gdn_c, mla_c], axis=-1) @ w_out[i])
    return rmsnorm(x, final_norm_w)
```

```python
import functools
import math

import jax
import jax.numpy as jnp
import numpy as np
from jax import lax
from jax.experimental import pallas as pl
from jax.experimental.pallas import tpu as pltpu

F32 = jnp.float32
BF16 = jnp.bfloat16

EPS = 1e-6
ROPE_BASE = 10000.0
GRID_W = 64

D_MODEL = 2048
SUB = 256
RET_HEADS = 4
GDN_HEADS = 4
HD = 128
GDN_CONV = 5
GDN_CHUNK = 64
MLA_HEADS = 8
MLA_ROPE = 64
MLA_Q_RANK = 768
MLA_KV_RANK = 512
MLA_QK = 256
MLA_SCALE = (HD + MLA_ROPE) ** -0.5

COL_RET = 0
COL_GDN = 2048
COL_CKV = 4096
COL_CQ = 4608
COL_MGATE = 5376
NZ = 6400
NZS = 128

VMEM_LIMIT = 56 * 1024 * 1024


def _cparams(sem):
    return pltpu.CompilerParams(dimension_semantics=sem, vmem_limit_bytes=VMEM_LIMIT)


def _silu(x):
    return x * (1.0 / (1.0 + jnp.exp(-x)))


def _dot(a, b):
    return jnp.dot(a, b, preferred_element_type=F32)


def _dot_nt(a, b):
    return lax.dot_general(a, b, (((1,), (1,)), ((), ())), preferred_element_type=F32)


def _dot_tn(a, b):
    return lax.dot_general(a, b, (((0,), (0,)), ((), ())), preferred_element_type=F32)


def _adaln_kernel(c_ref, w_ref, b_ref, o_ref):
    s = _silu(c_ref[...]).astype(BF16)
    o_ref[0] = _dot(s, w_ref[0].astype(BF16)) + b_ref[0]


def _adaln(cc, ada_w, ada_b):
    depth, d, n3 = ada_w.shape
    tn = 768
    return pl.pallas_call(
        _adaln_kernel,
        grid=(depth, n3 // tn),
        in_specs=[pl.BlockSpec((16, d), lambda i, j: (0, 0)),
                  pl.BlockSpec((1, d, tn), lambda i, j: (i, 0, j)),
                  pl.BlockSpec((1, 1, tn), lambda i, j: (i, 0, j))],
        out_specs=pl.BlockSpec((1, 16, tn), lambda i, j: (i, 0, j)),
        out_shape=jax.ShapeDtypeStruct((depth, 16, n3), F32),
        compiler_params=_cparams(("arbitrary", "arbitrary")),
        name="adaln",
    )(cc, ada_w, ada_b.reshape(depth, 1, n3))


def _mod_row(sub_idx, subs_per_batch):
    b = sub_idx // subs_per_batch
    is_lat = (sub_idx % subs_per_batch != 0).astype(jnp.int32)
    return b * 2 + is_lat


def _inproj_kernel(x_ref, ms_ref, nw_ref, w_ref, ws_ref, z_ref, zs_ref, h_ref, *, n_sub, subs_per_batch):
    i = pl.program_id(0)

    @pl.when(pl.program_id(1) == 0)
    def _():
        for u in range(n_sub):
            m = ms_ref[_mod_row(i * n_sub + u, subs_per_batch)]
            x = x_ref[u * SUB:(u + 1) * SUB, :]
            y = x * lax.rsqrt(jnp.mean(x * x, axis=-1, keepdims=True) + EPS) * nw_ref[...]
            h_ref[u * SUB:(u + 1) * SUB, :] = (y * (1.0 + m[0:1]) + m[1:2]).astype(BF16)
        zs_ref[...] = _dot(h_ref[...], ws_ref[...])

    z_ref[...] = _dot(h_ref[...], w_ref[...]).astype(BF16)


def _inproj(xa, ms, norm_w, w_main, w_small, subs_per_batch):
    bt, d = xa.shape
    tm = 1024 if bt % 1024 == 0 else SUB
    tn = 1280
    return pl.pallas_call(
        functools.partial(_inproj_kernel, n_sub=tm // SUB, subs_per_batch=subs_per_batch),
        grid=(bt // tm, NZ // tn),
        in_specs=[pl.BlockSpec((tm, d), lambda i, j: (i, 0)),
                  pl.BlockSpec(ms.shape, lambda i, j: (0, 0, 0)),
                  pl.BlockSpec((1, d), lambda i, j: (0, 0)),
                  pl.BlockSpec((d, tn), lambda i, j: (0, j)),
                  pl.BlockSpec((d, NZS), lambda i, j: (0, 0))],
        out_specs=[pl.BlockSpec((tm, tn), lambda i, j: (i, j)),
                   pl.BlockSpec((tm, NZS), lambda i, j: (i, 0))],
        out_shape=[jax.ShapeDtypeStruct((bt, NZ), BF16),
                   jax.ShapeDtypeStruct((bt, NZS), F32)],
        scratch_shapes=[pltpu.VMEM((tm, d), BF16)],
        compiler_params=_cparams(("arbitrary", "arbitrary")),
        name="inproj",
    )(xa, ms, norm_w.reshape(1, d), w_main, w_small)


def _outproj_kernel(x_ref, r_ref, g_ref, m_ref, gate_ref, w_ref, fw_ref, o_ref, *, n_sub, subs_per_batch, final):
    i = pl.program_id(0)
    acc = (_dot(r_ref[...], w_ref[0:512, :]) + _dot(g_ref[...], w_ref[512:1024, :])
           + _dot(m_ref[...], w_ref[1024:2048, :]))
    for u in range(n_sub):
        gate = gate_ref[_mod_row(i * n_sub + u, subs_per_batch)]
        rows = slice(u * SUB, (u + 1) * SUB)
        xn = x_ref[rows, :] + gate * acc[rows, :]
        if final:
            xn = xn * lax.rsqrt(jnp.mean(xn * xn, axis=-1, keepdims=True) + EPS) * fw_ref[...]
        o_ref[rows, :] = xn


def _outproj(xa, mret, mgdn, mmla, gate_tab, w_out, final_w, subs_per_batch, final):
    bt, d = xa.shape
    tm = 512 if bt % 512 == 0 else SUB
    return pl.pallas_call(
        functools.partial(_outproj_kernel, n_sub=tm // SUB, subs_per_batch=subs_per_batch, final=final),
        grid=(bt // tm,),
        in_specs=[pl.BlockSpec((tm, d), lambda i: (i, 0)),
                  pl.BlockSpec((tm, 512), lambda i: (i, 0)),
                  pl.BlockSpec((tm, 512), lambda i: (i, 0)),
                  pl.BlockSpec((tm, 1024), lambda i: (i, 0)),
                  pl.BlockSpec(gate_tab.shape, lambda i: (0, 0, 0)),
                  pl.BlockSpec((d, d), lambda i: (0, 0)),
                  pl.BlockSpec((1, d), lambda i: (0, 0))],
        out_specs=pl.BlockSpec((tm, d), lambda i: (i, 0)),
        out_shape=jax.ShapeDtypeStruct((bt, d), F32),
        input_output_aliases={0: 0},
        compiler_params=_cparams(("arbitrary",)),
        name="outproj",
    )(xa, mret, mgdn, mmla, gate_tab, w_out, final_w.reshape(1, d))


def _swap_halves(x, half):
    lane = lax.broadcasted_iota(jnp.int32, x.shape, x.ndim - 1)
    first = (lane % (2 * half)) < half
    return jnp.where(first, pltpu.roll(x, 128 - half, x.ndim - 1), pltpu.roll(x, half, x.ndim - 1))


def _rope_tables(n_ctx, seq, dim):
    rows = seq // GRID_W
    row = jnp.repeat(jnp.arange(rows), GRID_W).astype(F32)
    col = jnp.tile(jnp.arange(GRID_W), rows).astype(F32)
    m = dim // 2
    inv = ROPE_BASE ** (-jnp.arange(m // 2, dtype=F32) / (m // 2))
    ar, ac = row[:, None] * inv, col[:, None] * inv
    cos = jnp.concatenate([jnp.cos(ar), jnp.cos(ar), jnp.cos(ac), jnp.cos(ac)], axis=-1)
    sin = jnp.concatenate([-jnp.sin(ar), jnp.sin(ar), -jnp.sin(ac), jnp.sin(ac)], axis=-1)
    cos = jnp.pad(cos, ((n_ctx, 0), (0, 128 - dim)), constant_values=1.0)
    sin = jnp.pad(sin, ((n_ctx, 0), (0, 128 - dim)))
    return cos, sin


def _ret_kernel(dec_ref, q_ref, k_ref, v_ref, g_ref, cos_ref, sin_ref, nw_ref, o_ref,
                qs_ref, ks_ref, acc_ref, *, n_chunks):
    h = pl.program_id(1)
    gf = dec_ref[0, h]
    gb = dec_ref[1, h]
    c_len = SUB

    ri = lax.broadcasted_iota(jnp.int32, (c_len, c_len), 0)
    ci = lax.broadcasted_iota(jnp.int32, (c_len, c_len), 1)
    dist = (ri - ci).astype(F32)
    mask = (jnp.where(ri >= ci, jnp.exp(gf * jnp.maximum(dist, 0.0)), 0.0)
            + jnp.where(ri <= ci, jnp.exp(gb * jnp.maximum(-dist, 0.0)), 0.0))
    pos = lax.broadcasted_iota(jnp.int32, (c_len, HD), 0).astype(F32)
    q_dec_f = jnp.exp(gf * (pos + 1.0))
    q_dec_b = jnp.exp(gb * (c_len - pos))
    k_dec_f = jnp.exp(gf * (c_len - 1.0 - pos))
    k_dec_b = jnp.exp(gb * pos)
    s_dec_f = jnp.exp(gf * c_len)
    s_dec_b = jnp.exp(gb * c_len)

    def rows_of(c):
        return pl.ds(pl.multiple_of(c * c_len, c_len), c_len)

    def fwd(c, s_f):
        rows = rows_of(c)
        cos, sin = cos_ref[rows, :], sin_ref[rows, :]
        q = q_ref[0, rows, :].astype(F32)
        k = k_ref[0, rows, :].astype(F32)
        q = q * cos + _swap_halves(q, 32) * sin
        k = (k * cos + _swap_halves(k, 32) * sin) * (HD ** -0.5)
        qb, kb = q.astype(BF16), k.astype(BF16)
        qs_ref[rows, :] = q
        ks_ref[rows, :] = k
        v = v_ref[0, rows, :]
        p = (_dot_nt(qb, kb) * mask).astype(BF16)
        o = _dot(p, v) + _dot((q * q_dec_f).astype(BF16), s_f.astype(BF16))
        acc_ref[rows, :] = o
        return s_f * s_dec_f + _dot_tn((k * k_dec_f).astype(BF16), v)

    lax.fori_loop(0, n_chunks, fwd, jnp.zeros((HD, HD), F32))

    def finish(c, o):
        rows = rows_of(c)
        y = o * lax.rsqrt(jnp.mean(o * o, axis=-1, keepdims=True) + EPS) * nw_ref[...]
        o_ref[0, rows, :] = (_silu(g_ref[0, rows, :].astype(F32)) * y).astype(BF16)

    def bwd_state(c, s_b):
        rows = rows_of(c)
        return s_b * s_dec_b + _dot_tn((ks_ref[rows, :] * k_dec_b).astype(BF16), v_ref[0, rows, :])

    finish(0, acc_ref[rows_of(0), :])
    s_b0 = bwd_state(0, jnp.zeros((HD, HD), F32))

    def bwd(t, s_b):
        c = n_chunks - 1 - t
        rows = rows_of(c)
        o = acc_ref[rows, :] + _dot((qs_ref[rows, :] * q_dec_b).astype(BF16), s_b.astype(BF16))
        finish(c, o)
        return bwd_state(c, s_b)

    lax.fori_loop(0, n_chunks - 1, bwd, s_b0)


def _retention(z3, log_decay, norm_w, cos, sin):
    b, t, _ = z3.shape
    cb = COL_RET // HD

    def col(off):
        return pl.BlockSpec((1, t, HD), lambda i, h: (i, 0, cb + off + h))

    return pl.pallas_call(
        functools.partial(_ret_kernel, n_chunks=t // SUB),
        grid=(b, RET_HEADS),
        in_specs=[pl.BlockSpec(memory_space=pltpu.SMEM),
                  col(0), col(4), col(8), col(12),
                  pl.BlockSpec((t, HD), lambda i, h: (0, 0)),
                  pl.BlockSpec((t, HD), lambda i, h: (0, 0)),
                  pl.BlockSpec((1, HD), lambda i, h: (0, 0))],
        out_specs=pl.BlockSpec((1, t, HD), lambda i, h: (i, 0, h)),
        out_shape=jax.ShapeDtypeStruct((b, t, RET_HEADS * HD), BF16),
        scratch_shapes=[pltpu.VMEM((t, HD), F32), pltpu.VMEM((t, HD), F32), pltpu.VMEM((t, HD), F32)],
        compiler_params=_cparams(("arbitrary", "arbitrary")),
        name="retention",
    )(log_decay, z3, z3, z3, z3, cos, sin, norm_w.reshape(1, HD))


def _gdn_gate_kernel(a_ref, b_ref, alog_ref, dtb_ref, gcf_ref, gcb_ref, beta_ref):
    a = a_ref[0]
    x = a + dtb_ref[...]
    softplus = jnp.maximum(x, 0.0) + jnp.log(1.0 + jnp.exp(-jnp.abs(x)))
    g = -jnp.exp(alog_ref[...]) * softplus
    c = GDN_CHUNK
    si = lax.broadcasted_iota(jnp.int32, (c, c), 0)
    ji = lax.broadcasted_iota(jnp.int32, (c, c), 1)
    hi = lax.Precision.HIGHEST
    gcf_ref[0] = jnp.dot(g, (si <= ji).astype(F32), precision=hi, preferred_element_type=F32)
    gcb_ref[0] = jnp.dot(g, (si >= ji).astype(F32), precision=hi, preferred_element_type=F32)
    beta_ref[0] = 1.0 / (1.0 + jnp.exp(-b_ref[0]))


def _gdn_gates(a_rows, b_rows, alog_rows, dtb_rows):
    b, r, c = a_rows.shape
    spec = pl.BlockSpec((1, r, c), lambda i: (i, 0, 0))
    pspec = pl.BlockSpec((r, 1), lambda i: (0, 0))
    shp = jax.ShapeDtypeStruct((b, r, c), F32)
    return pl.pallas_call(
        _gdn_gate_kernel,
        grid=(b,),
        in_specs=[spec, spec, pspec, pspec],
        out_specs=[spec, spec, spec],
        out_shape=[shp, shp, shp],
        compiler_params=_cparams(("arbitrary",)),
        name="gdn_gates",
    )(a_rows, b_rows, alog_rows, dtb_rows)


GDN_GROUP = 4


def _gdn_kernel(q_ref, k_ref, v_ref, g_ref, cwq_ref, cwk_ref, cwv_ref, col_ref, row_ref, nw_ref, o_ref,
                pad_ref, q32_ref, k32_ref, v32_ref, w_ref, u_ref, qe_ref, kd_ref, att_ref, eg_ref, acc_ref,
                *, n_tok):
    c_len = GDN_CHUNK
    n_chunks = n_tok // c_len
    n_sub = n_tok // SUB
    grp = GDN_GROUP
    halo = 8

    pad_ref[0:halo, :] = jnp.zeros((halo, HD), F32)
    pad_ref[halo + n_tok:halo + n_tok + halo, :] = jnp.zeros((halo, HD), F32)
    win_row = lax.broadcasted_iota(jnp.int32, (SUB + 2 * halo, HD), 0)

    for src_ref, cw_ref, dst_ref, norm_scale in ((q_ref, cwq_ref, q32_ref, HD ** -0.5),
                                                 (k_ref, cwk_ref, k32_ref, 1.0),
                                                 (v_ref, cwv_ref, v32_ref, None)):
        def fill(s, _):
            rows = pl.ds(pl.multiple_of(s * SUB, SUB), SUB)
            pad_ref[pl.ds(pl.multiple_of(halo + s * SUB, 8), SUB), :] = src_ref[0, rows, :].astype(F32)
            return 0

        lax.fori_loop(0, n_sub, fill, 0)
        cw = cw_ref[...]

        def conv(s, _):
            win = pad_ref[pl.ds(pl.multiple_of(s * SUB, SUB), SUB + 2 * halo), :]
            tok = win_row + (s * SUB - halo)
            same = (tok >= SUB).astype(jnp.int32) == jnp.minimum(s, 1)
            win = jnp.where(same, win, 0.0)
            y = jnp.zeros((SUB, HD), F32)
            for j in range(GDN_CONV):
                off = halo + j - GDN_CONV // 2
                y = y + win[off:off + SUB, :] * cw[j:j + 1, :]
            y = _silu(y)
            if norm_scale is not None:
                y = y * (lax.rsqrt(jnp.sum(y * y, axis=-1, keepdims=True) + EPS) * norm_scale)
            dst_ref[pl.ds(pl.multiple_of(s * SUB, SUB), SUB), :] = y
            return 0

        lax.fori_loop(0, n_sub, conv, 0)

    ri = lax.broadcasted_iota(jnp.int32, (grp, c_len, c_len), 1)
    ci = lax.broadcasted_iota(jnp.int32, (grp, c_len, c_len), 2)
    eye = (ri == ci).astype(F32)
    tri = (ri >= ci, ri <= ci)
    strict = (ri > ci, ri < ci)
    merge_masks = []
    m = 1
    while m < c_len:
        merge_masks.append(((ri // (2 * m)) == (ci // (2 * m))) & ((ri // m) != (ci // m)))
        m *= 2

    def bmm(a, b):
        return lax.dot_general(a, b, (((2,), (1,)), ((0,), (0,))), preferred_element_type=F32)

    def bmm_nt(a, b):
        return lax.dot_general(a, b, (((2,), (2,)), ((0,), (0,))), preferred_element_type=F32)

    def local(gi, _):
        r0 = pl.multiple_of(gi * (grp * c_len), grp * c_len)
        rows = pl.ds(r0, grp * c_len)
        q = q32_ref[rows, :].reshape(grp, c_len, HD)
        k = k32_ref[rows, :].reshape(grp, c_len, HD)
        v = v32_ref[rows, :].reshape(grp, c_len, HD)
        kb16 = k.astype(BF16)
        qk = bmm_nt(q.astype(BF16), kb16)
        cols = col_ref[0, 0, rows, :].reshape(grp, c_len, 8)
        for d in range(2):
            gc = cols[:, :, d:d + 1]
            beta = cols[:, :, 2 + d:3 + d]
            gc_row = jnp.stack([row_ref[0, 0, d, pl.ds(gi * grp + g, 1), :] for g in range(grp)], axis=0)
            g_tot = gc[:, c_len - 1:c_len, :] if d == 0 else gc[:, 0:1, :]
            decay = jnp.exp(jnp.where(tri[d], gc - gc_row, -jnp.inf))
            kbeta = k * beta
            a = jnp.where(strict[d], bmm_nt(kbeta.astype(BF16), kb16) * decay, 0.0)
            t = eye - jnp.where(merge_masks[0], a, 0.0)
            for mm in merge_masks[1:]:
                t16 = t.astype(BF16)
                t = t - bmm(bmm(t16, jnp.where(mm, a, 0.0).astype(BF16)).astype(BF16), t16)
            rhs = jnp.concatenate([v * beta, kbeta * jnp.exp(gc)], axis=-1).astype(BF16)
            sol = bmm(t.astype(BF16), rhs)
            u_ref[d, rows, :] = sol[:, :, :HD].reshape(grp * c_len, HD)
            w_ref[d, rows, :] = sol[:, :, HD:].reshape(grp * c_len, HD).astype(BF16)
            qe_ref[d, rows, :] = (q * jnp.exp(gc)).reshape(grp * c_len, HD).astype(BF16)
            kd_ref[d, rows, :] = (k * jnp.exp(g_tot - gc)).reshape(grp * c_len, HD).astype(BF16)
            att_ref[d, rows, :] = (qk * decay).reshape(grp * c_len, c_len).astype(BF16)
            eg_ref[d, pl.ds(pl.multiple_of(gi * grp, grp), grp), :] = jnp.broadcast_to(
                jnp.exp(g_tot).reshape(grp, 1), (grp, HD))
        return 0

    lax.fori_loop(0, n_chunks // grp, local, 0)

    ctx_chunks = SUB // c_len

    def step(d, c, s):
        rows = pl.ds(pl.multiple_of(c * c_len, c_len), c_len)
        s16 = s.astype(BF16)
        v_new = (u_ref[d, rows, :] - _dot(w_ref[d, rows, :], s16)).astype(BF16)
        o = _dot(qe_ref[d, rows, :], s16) + _dot(att_ref[d, rows, :], v_new)
        s = s * eg_ref[d, pl.ds(c, 1), :] + _dot_tn(kd_ref[d, rows, :], v_new)
        return o, s

    s0 = jnp.zeros((HD, HD), F32)

    def scan_f(i, s_f):
        o_f, s_f = step(0, i, s_f)
        acc_ref[pl.ds(pl.multiple_of(i * c_len, c_len), c_len), :] = o_f
        return s_f

    lax.fori_loop(0, n_chunks, scan_f, s0)

    def scan_b(i, s_b):
        c = jnp.where(i < ctx_chunks, ctx_chunks - 1 - i, n_chunks - 1 + ctx_chunks - i)
        o_b, s_b = step(1, c, s_b)
        rows = pl.ds(pl.multiple_of(c * c_len, c_len), c_len)
        acc_ref[rows, :] = acc_ref[rows, :] + o_b
        return s_b

    lax.fori_loop(0, n_chunks, scan_b, s0)

    def finish(s, _):
        rows = pl.ds(pl.multiple_of(s * SUB, SUB), SUB)
        o = acc_ref[rows, :]
        y = o * lax.rsqrt(jnp.mean(o * o, axis=-1, keepdims=True) + EPS) * nw_ref[...]
        o_ref[0, rows, :] = (_silu(g_ref[0, rows, :].astype(F32)) * y).astype(BF16)
        return 0

    lax.fori_loop(0, n_sub, finish, 0)


def _gdn(z3, conv_w, cols, rows, norm_w):
    b, t, _ = z3.shape
    cb = COL_GDN // HD
    n_chunks = t // GDN_CHUNK

    def col(off):
        return pl.BlockSpec((1, t, HD), lambda i, h: (i, 0, cb + off + h))

    def cw(off):
        return pl.BlockSpec((GDN_CONV, HD), lambda i, h: (0, off + h))

    return pl.pallas_call(
        functools.partial(_gdn_kernel, n_tok=t),
        grid=(b, GDN_HEADS),
        in_specs=[col(0), col(4), col(8), col(12), cw(0), cw(4), cw(8),
                  pl.BlockSpec((1, 1, t, 8), lambda i, h: (i, h, 0, 0)),
                  pl.BlockSpec((1, 1, 2, n_chunks, GDN_CHUNK), lambda i, h: (i, h, 0, 0, 0)),
                  pl.BlockSpec((1, HD), lambda i, h: (0, 0))],
        out_specs=pl.BlockSpec((1, t, HD), lambda i, h: (i, 0, h)),
        out_shape=jax.ShapeDtypeStruct((b, t, GDN_HEADS * HD), BF16),
        scratch_shapes=[pltpu.VMEM((t + 16, HD), F32),
                        pltpu.VMEM((t, HD), F32), pltpu.VMEM((t, HD), F32), pltpu.VMEM((t, HD), F32),
                        pltpu.VMEM((2, t, HD), BF16), pltpu.VMEM((2, t, HD), F32),
                        pltpu.VMEM((2, t, HD), BF16), pltpu.VMEM((2, t, HD), BF16),
                        pltpu.VMEM((2, t, GDN_CHUNK), BF16),
                        pltpu.VMEM((2, n_chunks, HD), F32),
                        pltpu.VMEM((t, HD), F32)],
        compiler_params=_cparams(("arbitrary", "arbitrary")),
        name="gdn",
    )(z3, z3, z3, z3, conv_w, conv_w, conv_w, cols, rows, norm_w.reshape(1, HD))


def _mla_prep_kernel(cq_ref, ckv_ref, zs_ref, qnw_ref, kvnw_ref, wq_ref, wkv_ref, cos_ref, sin_ref,
                     q_ref, k_ref, v_ref):
    def rms(x, w):
        return (x * lax.rsqrt(jnp.mean(x * x, axis=-1, keepdims=True) + EPS) * w).astype(BF16)

    cos, sin = cos_ref[...], sin_ref[...]

    def rope(x):
        return x * cos + _swap_halves(x, 16) * sin

    q = _dot(rms(cq_ref[...].astype(F32), qnw_ref[...]), wq_ref[...])
    kv = _dot(rms(ckv_ref[...].astype(F32), kvnw_ref[...]), wkv_ref[...])
    lane = lax.broadcasted_iota(jnp.int32, (SUB, HD), 1)
    kr = rope(jnp.where(lane < MLA_ROPE, zs_ref[...], 0.0)).astype(BF16)
    for h in range(MLA_HEADS):
        base = h * MLA_QK
        q_ref[0, h, :, 0:HD] = (q[:, base:base + HD] * MLA_SCALE).astype(BF16)
        q_ref[0, h, :, HD:MLA_QK] = (rope(q[:, base + HD:base + MLA_QK]) * MLA_SCALE).astype(BF16)
        k_ref[0, h, :, 0:HD] = kv[:, h * HD:(h + 1) * HD].astype(BF16)
        k_ref[0, h, :, HD:MLA_QK] = kr
        v_ref[0, h] = kv[:, (MLA_HEADS + h) * HD:(MLA_HEADS + h + 1) * HD].astype(BF16)


def _mla_prep(z, zs, qnw, kvnw, wq, wkv, cos, sin, b, t):
    spb = t // SUB

    def tok(width, blk):
        return pl.BlockSpec((SUB, width), lambda i: (i, blk))

    def par(shape):
        return pl.BlockSpec(shape, lambda i: (0, 0))

    def head_out(width):
        return pl.BlockSpec((1, MLA_HEADS, SUB, width), lambda i: (i // spb, 0, i % spb, 0))

    return pl.pallas_call(
        _mla_prep_kernel,
        grid=(b * spb,),
        in_specs=[tok(MLA_Q_RANK, COL_CQ // MLA_Q_RANK), tok(MLA_KV_RANK, COL_CKV // MLA_KV_RANK), tok(NZS, 0),
                  par((1, MLA_Q_RANK)), par((1, MLA_KV_RANK)),
                  par(wq.shape), par(wkv.shape),
                  pl.BlockSpec((SUB, HD), lambda i: (i % spb, 0)),
                  pl.BlockSpec((SUB, HD), lambda i: (i % spb, 0))],
        out_specs=[head_out(MLA_QK), head_out(MLA_QK), head_out(HD)],
        out_shape=[jax.ShapeDtypeStruct((b, MLA_HEADS, t, MLA_QK), BF16),
                   jax.ShapeDtypeStruct((b, MLA_HEADS, t, MLA_QK), BF16),
                   jax.ShapeDtypeStruct((b, MLA_HEADS, t, HD), BF16)],
        compiler_params=_cparams(("arbitrary",)),
        name="mla_prep",
    )(z, z, zs, qnw.reshape(1, -1), kvnw.reshape(1, -1), wq, wkv, cos, sin)


def _mla_attn_kernel(q_ref, k_ref, v_ref, g_ref, o_ref):
    q = q_ref[0, 0]
    gate = _silu(g_ref[0].astype(F32))

    def attend(n_keys):
        s = _dot_nt(q, k_ref[0, 0, 0:n_keys, :])
        p = jnp.exp(s - jnp.max(s, axis=-1, keepdims=True))
        l = jnp.sum(p, axis=-1, keepdims=True)
        o = _dot(p.astype(BF16), v_ref[0, 0, 0:n_keys, :]) / l
        o_ref[0] = (gate * o).astype(BF16)

    @pl.when(pl.program_id(2) == 0)
    def _():
        attend(SUB)

    @pl.when(pl.program_id(2) > 0)
    def _():
        attend(k_ref.shape[2])


def _mla_attn(qp, kp, vp, z3):
    b, nh, t, _ = qp.shape
    gb = COL_MGATE // HD
    return pl.pallas_call(
        _mla_attn_kernel,
        grid=(b, nh, t // SUB),
        in_specs=[pl.BlockSpec((1, 1, SUB, MLA_QK), lambda i, h, j: (i, h, j, 0)),
                  pl.BlockSpec((1, 1, t, MLA_QK), lambda i, h, j: (i, h, 0, 0)),
                  pl.BlockSpec((1, 1, t, HD), lambda i, h, j: (i, h, 0, 0)),
                  pl.BlockSpec((1, SUB, HD), lambda i, h, j: (i, j, gb + h))],
        out_specs=pl.BlockSpec((1, SUB, HD), lambda i, h, j: (i, j, h)),
        out_shape=jax.ShapeDtypeStruct((b, t, nh * HD), BF16),
        compiler_params=_cparams(("arbitrary", "arbitrary", "arbitrary")),
        name="mla_attn",
    )(qp, kp, vp, z3)


def _layout_w_in(w_in):
    sizes = (512, 512, 512, 512, 512, 512, 512, 512, 8, 8, MLA_Q_RANK, MLA_KV_RANK, MLA_ROPE, 1024)
    offs = np.concatenate([[0], np.cumsum(sizes)])
    part = lambda n: w_in[:, offs[n]:offs[n + 1]]
    w_main = jnp.concatenate([part(n) for n in (0, 1, 2, 3, 4, 5, 6, 7, 11, 10, 13)], axis=-1)
    w_small = jnp.concatenate([part(12), part(8), part(9)], axis=-1)
    w_small = jnp.pad(w_small, ((0, 0), (0, NZS - w_small.shape[-1])))
    return w_main.astype(BF16), w_small.astype(BF16)


def _layout_w_qb(w_qb):
    w = w_qb.reshape(MLA_Q_RANK, MLA_HEADS, HD + MLA_ROPE)
    w = jnp.pad(w, ((0, 0), (0, 0), (0, MLA_QK - HD - MLA_ROPE)))
    return w.reshape(MLA_Q_RANK, MLA_HEADS * MLA_QK).astype(BF16)


def _layout_w_kvb(w_kvb):
    w = w_kvb.reshape(MLA_KV_RANK, MLA_HEADS, 2, HD)
    return jnp.transpose(w, (0, 2, 1, 3)).reshape(MLA_KV_RANK, 2 * MLA_HEADS * HD).astype(BF16)


def kernel(x, c, ctx, c_ctx, ada_w, ada_b, norm_w, w_in, ret_log_decay, ret_norm_w, gdn_conv_w, gdn_a_log,
           gdn_dt_bias, gdn_norm_w, mla_q_norm_w, mla_w_qb, mla_kv_norm_w, mla_w_kvb, w_out, final_norm_w):
    b, seq, d = x.shape
    n_ctx = ctx.shape[1]
    assert n_ctx == SUB and seq % SUB == 0 and d == D_MODEL
    depth = ada_w.shape[0]
    t = n_ctx + seq
    spb = t // SUB
    n_chunks = t // GDN_CHUNK

    cc = jnp.zeros((16, d), F32).at[:b].set(c).at[b].set(c_ctx)
    mod = _adaln(cc, ada_w, ada_b)
    sel = np.stack([np.full(b, b), np.arange(b)], axis=1).reshape(-1)

    ret_cos, ret_sin = _rope_tables(n_ctx, seq, HD)
    mla_cos, mla_sin = _rope_tables(n_ctx, seq, MLA_ROPE)

    xa = jnp.concatenate([ctx, x], axis=1).reshape(b * t, d)
    for i in range(depth):
        final = i == depth - 1
        shift, scale, gate = mod[i, :, :d], mod[i, :, d:2 * d], mod[i, :, 2 * d:]
        ms = jnp.stack([scale[sel], shift[sel]], axis=1)
        gate_tab = gate[sel][:, None, :]
        w_main, w_small = _layout_w_in(w_in[i])

        z, zs = _inproj(xa, ms, norm_w[i], w_main, w_small, spb)
        z3 = z.reshape(b, t, NZ)

        m_ret = _retention(z3, ret_log_decay[i], ret_norm_w[i], ret_cos, ret_sin)

        zs3 = zs.reshape(b, t, NZS)
        to_rows = lambda a: jnp.transpose(a, (0, 2, 1)).reshape(b, 8 * n_chunks, GDN_CHUNK)
        per_row = lambda p: jnp.repeat(p.reshape(8), n_chunks).reshape(8 * n_chunks, 1)
        gcf, gcb, beta = _gdn_gates(to_rows(zs3[:, :, 64:72]), to_rows(zs3[:, :, 72:80]),
                                    per_row(gdn_a_log[i]), per_row(gdn_dt_bias[i]))
        r5 = lambda a: a.reshape(b, 2, GDN_HEADS, n_chunks, GDN_CHUNK)
        gc = jnp.stack([r5(gcf)[:, 0], r5(gcb)[:, 1]], axis=2)
        bt5 = jnp.transpose(r5(beta), (0, 2, 1, 3, 4))
        cols = jnp.concatenate([gc, bt5], axis=2).reshape(b, GDN_HEADS, 4, t)
        cols = jnp.pad(jnp.transpose(cols, (0, 1, 3, 2)), ((0, 0), (0, 0), (0, 0), (0, 4)))
        m_gdn = _gdn(z3, gdn_conv_w[i], cols, gc, gdn_norm_w[i])

        qp, kp, vp = _mla_prep(z, zs, mla_q_norm_w[i], mla_kv_norm_w[i], _layout_w_qb(mla_w_qb[i]),
                               _layout_w_kvb(mla_w_kvb[i]), mla_cos, mla_sin, b, t)
        m_mla = _mla_attn(qp, kp, vp, z3)

        xa = _outproj(xa, m_ret.reshape(b * t, -1), m_gdn.reshape(b * t, -1), m_mla.reshape(b * t, -1),
                      gate_tab, w_out[i].astype(BF16), final_norm_w, spb, final)
    return xa.reshape(b, t, d)[:, n_ctx:]
```

```python
import functools
import math

import jax
import jax.numpy as jnp
import numpy as np
from jax import lax
from jax.experimental import pallas as pl
from jax.experimental.pallas import tpu as pltpu

F32 = jnp.float32
BF16 = jnp.bfloat16

EPS = 1e-6
ROPE_BASE = 10000.0
GRID_W = 64

D_MODEL = 2048
SUB = 256
RET_HEADS = 4
GDN_HEADS = 4
HD = 128
GDN_CONV = 5
GDN_CHUNK = 64
MLA_HEADS = 8
MLA_ROPE = 64
MLA_Q_RANK = 768
MLA_KV_RANK = 512
MLA_QK = 256
MLA_SCALE = (HD + MLA_ROPE) ** -0.5
MLA_Q_SCALE = MLA_SCALE * math.log2(math.e)

COL_RET = 0
COL_GDN = 2048
COL_CKV = 4096
COL_CQ = 4608
COL_MGATE = 5376
NZ = 6400
NZS = 128

VMEM_LIMIT = 56 * 1024 * 1024


def _cparams(sem):
    return pltpu.CompilerParams(dimension_semantics=sem, vmem_limit_bytes=VMEM_LIMIT)


def _silu(x):
    return x * (1.0 / (1.0 + jnp.exp(-x)))


def _dot(a, b):
    return jnp.dot(a, b, preferred_element_type=F32)


def _dot_nt(a, b):
    return lax.dot_general(a, b, (((1,), (1,)), ((), ())), preferred_element_type=F32)


def _dot_tn(a, b):
    return lax.dot_general(a, b, (((0,), (0,)), ((), ())), preferred_element_type=F32)


def _adaln_kernel(c_ref, w_ref, b_ref, o_ref):
    s = _silu(c_ref[...]).astype(BF16)
    o_ref[0] = _dot(s, w_ref[0].astype(BF16)) + b_ref[0]


def _adaln(cc, ada_w, ada_b):
    depth, d, n3 = ada_w.shape
    tn = 768
    return pl.pallas_call(
        _adaln_kernel,
        grid=(depth, n3 // tn),
        in_specs=[pl.BlockSpec((16, d), lambda i, j: (0, 0)),
                  pl.BlockSpec((1, d, tn), lambda i, j: (i, 0, j)),
                  pl.BlockSpec((1, 1, tn), lambda i, j: (i, 0, j))],
        out_specs=pl.BlockSpec((1, 16, tn), lambda i, j: (i, 0, j)),
        out_shape=jax.ShapeDtypeStruct((depth, 16, n3), F32),
        compiler_params=_cparams(("arbitrary", "arbitrary")),
        name="adaln",
    )(cc, ada_w, ada_b.reshape(depth, 1, n3))


def _mod_row(sub_idx, subs_per_batch):
    b = sub_idx // subs_per_batch
    is_lat = (sub_idx % subs_per_batch != 0).astype(jnp.int32)
    return b * 2 + is_lat


def _inproj_kernel(x_ref, ms_ref, nw_ref, w_ref, ws_ref, z_ref, zs_ref, h_ref, *, n_sub, subs_per_batch):
    i = pl.program_id(0)

    @pl.when(pl.program_id(1) == 0)
    def _():
        for u in range(n_sub):
            m = ms_ref[_mod_row(i * n_sub + u, subs_per_batch)]
            x = x_ref[u * SUB:(u + 1) * SUB, :]
            y = x * lax.rsqrt(jnp.mean(x * x, axis=-1, keepdims=True) + EPS) * nw_ref[...]
            h_ref[u * SUB:(u + 1) * SUB, :] = (y * (1.0 + m[0:1]) + m[1:2]).astype(BF16)
        zs_ref[...] = _dot(h_ref[...], ws_ref[...])

    z_ref[...] = _dot(h_ref[...], w_ref[...]).astype(BF16)


def _inproj(xa, ms, norm_w, w_main, w_small, subs_per_batch):
    bt, d = xa.shape
    tm = 1024 if bt % 1024 == 0 else SUB
    tn = 1280
    return pl.pallas_call(
        functools.partial(_inproj_kernel, n_sub=tm // SUB, subs_per_batch=subs_per_batch),
        grid=(bt // tm, NZ // tn),
        in_specs=[pl.BlockSpec((tm, d), lambda i, j: (i, 0)),
                  pl.BlockSpec(ms.shape, lambda i, j: (0, 0, 0)),
                  pl.BlockSpec((1, d), lambda i, j: (0, 0)),
                  pl.BlockSpec((d, tn), lambda i, j: (0, j)),
                  pl.BlockSpec((d, NZS), lambda i, j: (0, 0))],
        out_specs=[pl.BlockSpec((tm, tn), lambda i, j: (i, j)),
                   pl.BlockSpec((tm, NZS), lambda i, j: (i, 0))],
        out_shape=[jax.ShapeDtypeStruct((bt, NZ), BF16),
                   jax.ShapeDtypeStruct((bt, NZS), F32)],
        scratch_shapes=[pltpu.VMEM((tm, d), BF16)],
        compiler_params=_cparams(("arbitrary", "arbitrary")),
        name="inproj",
    )(xa, ms, norm_w.reshape(1, d), w_main, w_small)


def _outproj_kernel(x_ref, r_ref, g_ref, m_ref, gate_ref, w_ref, fw_ref, o_ref, *, n_sub, subs_per_batch, final):
    i = pl.program_id(0)
    acc = (_dot(r_ref[...], w_ref[0:512, :]) + _dot(g_ref[...], w_ref[512:1024, :])
           + _dot(m_ref[...], w_ref[1024:2048, :]))
    for u in range(n_sub):
        row = (i // (subs_per_batch - 1)) * 2 + 1 if final else _mod_row(i * n_sub + u, subs_per_batch)
        gate = gate_ref[row]
        rows = slice(u * SUB, (u + 1) * SUB)
        xn = x_ref[rows, :] + gate * acc[rows, :]
        if final:
            xn = xn * lax.rsqrt(jnp.mean(xn * xn, axis=-1, keepdims=True) + EPS) * fw_ref[...]
        o_ref[rows, :] = xn


def _outproj(xa, mret, mgdn, mmla, gate_tab, w_out, final_w, subs_per_batch, final):
    bt, d = xa.shape
    if final:
        tm, lat = SUB, subs_per_batch - 1
        n_blocks = (bt // SUB // subs_per_batch) * lat
        src = lambda i: ((i // lat) * subs_per_batch + 1 + i % lat, 0)
        aliases = {}
    else:
        tm = 512 if bt % 512 == 0 else SUB
        n_blocks = bt // tm
        src = lambda i: (i, 0)
        aliases = {0: 0}
    return pl.pallas_call(
        functools.partial(_outproj_kernel, n_sub=tm // SUB, subs_per_batch=subs_per_batch, final=final),
        grid=(n_blocks,),
        in_specs=[pl.BlockSpec((tm, d), src),
                  pl.BlockSpec((tm, 512), src),
                  pl.BlockSpec((tm, 512), src),
                  pl.BlockSpec((tm, 1024), src),
                  pl.BlockSpec(gate_tab.shape, lambda i: (0, 0, 0)),
                  pl.BlockSpec((d, d), lambda i: (0, 0)),
                  pl.BlockSpec((1, d), lambda i: (0, 0))],
        out_specs=pl.BlockSpec((tm, d), lambda i: (i, 0)),
        out_shape=jax.ShapeDtypeStruct((n_blocks * tm, d), F32),
        input_output_aliases=aliases,
        compiler_params=_cparams(("arbitrary",)),
        name="outproj",
    )(xa, mret, mgdn, mmla, gate_tab, w_out, final_w.reshape(1, d))


def _swap_halves(x, half):
    lane = lax.broadcasted_iota(jnp.int32, x.shape, x.ndim - 1)
    first = (lane % (2 * half)) < half
    return jnp.where(first, pltpu.roll(x, 128 - half, x.ndim - 1), pltpu.roll(x, half, x.ndim - 1))


def _rope_tables(n_ctx, seq, dim):
    rows = seq // GRID_W
    row = jnp.repeat(jnp.arange(rows), GRID_W).astype(F32)
    col = jnp.tile(jnp.arange(GRID_W), rows).astype(F32)
    m = dim // 2
    inv = ROPE_BASE ** (-jnp.arange(m // 2, dtype=F32) / (m // 2))
    ar, ac = row[:, None] * inv, col[:, None] * inv
    cos = jnp.concatenate([jnp.cos(ar), jnp.cos(ar), jnp.cos(ac), jnp.cos(ac)], axis=-1)
    sin = jnp.concatenate([-jnp.sin(ar), jnp.sin(ar), -jnp.sin(ac), jnp.sin(ac)], axis=-1)
    cos = jnp.pad(cos, ((n_ctx, 0), (0, 128 - dim)), constant_values=1.0)
    sin = jnp.pad(sin, ((n_ctx, 0), (0, 128 - dim)))
    return cos, sin


def _ret_kernel(dec_ref, q_ref, k_ref, v_ref, g_ref, cos_ref, sin_ref, nw_ref, o_ref,
                qs_ref, ks_ref, acc_ref, *, n_chunks):
    h = pl.program_id(1)
    gf = dec_ref[0, h]
    gb = dec_ref[1, h]
    c_len = SUB

    ri = lax.broadcasted_iota(jnp.int32, (c_len, c_len), 0)
    ci = lax.broadcasted_iota(jnp.int32, (c_len, c_len), 1)
    dist = (ri - ci).astype(F32)
    mask = (jnp.where(ri >= ci, jnp.exp(gf * jnp.maximum(dist, 0.0)), 0.0)
            + jnp.where(ri <= ci, jnp.exp(gb * jnp.maximum(-dist, 0.0)), 0.0))
    pos = lax.broadcasted_iota(jnp.int32, (c_len, HD), 0).astype(F32)
    q_dec_f = jnp.exp(gf * (pos + 1.0))
    q_dec_b = jnp.exp(gb * (c_len - pos))
    k_dec_f = jnp.exp(gf * (c_len - 1.0 - pos))
    k_dec_b = jnp.exp(gb * pos)
    s_dec_f = jnp.exp(gf * c_len)
    s_dec_b = jnp.exp(gb * c_len)

    def rows_of(c):
        return pl.ds(pl.multiple_of(c * c_len, c_len), c_len)

    def fwd(c, s_f):
        rows = rows_of(c)
        cos, sin = cos_ref[rows, :], sin_ref[rows, :]
        q = q_ref[0, rows, :].astype(F32)
        k = k_ref[0, rows, :].astype(F32)
        q = q * cos + _swap_halves(q, 32) * sin
        k = (k * cos + _swap_halves(k, 32) * sin) * (HD ** -0.5)
        qb, kb = q.astype(BF16), k.astype(BF16)
        qs_ref[rows, :] = q
        ks_ref[rows, :] = k
        v = v_ref[0, rows, :]
        p = (_dot_nt(qb, kb) * mask).astype(BF16)
        o = _dot(p, v) + _dot((q * q_dec_f).astype(BF16), s_f.astype(BF16))
        acc_ref[rows, :] = o
        return s_f * s_dec_f + _dot_tn((k * k_dec_f).astype(BF16), v)

    lax.fori_loop(0, n_chunks, fwd, jnp.zeros((HD, HD), F32))

    def finish(c, o):
        rows = rows_of(c)
        y = o * lax.rsqrt(jnp.mean(o * o, axis=-1, keepdims=True) + EPS) * nw_ref[...]
        o_ref[0, rows, :] = (_silu(g_ref[0, rows, :].astype(F32)) * y).astype(BF16)

    def bwd_state(c, s_b):
        rows = rows_of(c)
        return s_b * s_dec_b + _dot_tn((ks_ref[rows, :] * k_dec_b).astype(BF16), v_ref[0, rows, :])

    finish(0, acc_ref[rows_of(0), :])
    s_b0 = bwd_state(0, jnp.zeros((HD, HD), F32))

    def bwd(t, s_b):
        c = n_chunks - 1 - t
        rows = rows_of(c)
        o = acc_ref[rows, :] + _dot((qs_ref[rows, :] * q_dec_b).astype(BF16), s_b.astype(BF16))
        finish(c, o)
        return bwd_state(c, s_b)

    lax.fori_loop(0, n_chunks - 1, bwd, s_b0)


def _retention(z3, log_decay, norm_w, cos, sin):
    b, t, _ = z3.shape
    cb = COL_RET // HD

    def col(off):
        return pl.BlockSpec((1, t, HD), lambda i, h: (i, 0, cb + off + h))

    return pl.pallas_call(
        functools.partial(_ret_kernel, n_chunks=t // SUB),
        grid=(b, RET_HEADS),
        in_specs=[pl.BlockSpec(memory_space=pltpu.SMEM),
                  col(0), col(4), col(8), col(12),
                  pl.BlockSpec((t, HD), lambda i, h: (0, 0)),
                  pl.BlockSpec((t, HD), lambda i, h: (0, 0)),
                  pl.BlockSpec((1, HD), lambda i, h: (0, 0))],
        out_specs=pl.BlockSpec((1, t, HD), lambda i, h: (i, 0, h)),
        out_shape=jax.ShapeDtypeStruct((b, t, RET_HEADS * HD), BF16),
        scratch_shapes=[pltpu.VMEM((t, HD), F32), pltpu.VMEM((t, HD), F32), pltpu.VMEM((t, HD), F32)],
        compiler_params=_cparams(("arbitrary", "arbitrary")),
        name="retention",
    )(log_decay, z3, z3, z3, z3, cos, sin, norm_w.reshape(1, HD))


LANE_GC = 64
LANE_BETA = 72


def _gdn_gate_kernel(zs_ref, alog_ref, dtb_ref, col_ref, row_ref):
    x = zs_ref[...]
    z = x + dtb_ref[...]
    softplus = jnp.maximum(z, 0.0) + jnp.log(1.0 + jnp.exp(-jnp.abs(z)))
    g = -jnp.exp(alog_ref[...]) * softplus
    c = GDN_CHUNK
    ri = lax.broadcasted_iota(jnp.int32, (c, c), 0)
    ci = lax.broadcasted_iota(jnp.int32, (c, c), 1)
    lower, upper = (ri >= ci).astype(F32), (ri <= ci).astype(F32)
    lane = lax.broadcasted_iota(jnp.int32, (c, NZS), 1)
    hi = lax.Precision.HIGHEST
    for j in range(SUB // c):
        gj = g[j * c:(j + 1) * c, :]
        pre = jnp.dot(lower, gj, precision=hi, preferred_element_type=F32)
        suf = jnp.dot(upper, gj, precision=hi, preferred_element_type=F32)
        beta = 1.0 / (1.0 + jnp.exp(-x[j * c:(j + 1) * c, :]))
        col_ref[j * c:(j + 1) * c, :] = jnp.where(lane < LANE_GC + GDN_HEADS, pre, jnp.where(lane < LANE_BETA, suf, beta))
    row_ref[0] = col_ref[...].T


def _gdn_gates(zs, alog_lane, dtb_lane, b, t):
    spb = t // SUB
    return pl.pallas_call(
        _gdn_gate_kernel,
        grid=(b * spb,),
        in_specs=[pl.BlockSpec((SUB, NZS), lambda i: (i, 0)),
                  pl.BlockSpec((1, NZS), lambda i: (0, 0)),
                  pl.BlockSpec((1, NZS), lambda i: (0, 0))],
        out_specs=[pl.BlockSpec((SUB, NZS), lambda i: (i, 0)),
                   pl.BlockSpec((1, NZS, SUB), lambda i: (i // spb, 0, i % spb))],
        out_shape=[jax.ShapeDtypeStruct((b * t, NZS), F32),
                   jax.ShapeDtypeStruct((b, NZS, t), F32)],
        compiler_params=_cparams(("arbitrary",)),
        name="gdn_gates",
    )(zs, alog_lane, dtb_lane)


GDN_GROUP = 17


def _gdn_kernel(q_ref, k_ref, v_ref, g_ref, cwq_ref, cwk_ref, cwv_ref, col_ref, row_ref, nw_ref, o_ref,
                pad_ref, q32_ref, k32_ref, v32_ref, w_ref, u_ref, qe_ref, sq_ref, sm_ref, att_ref, eg_ref, acc_ref,
                *, n_tok):
    head = pl.program_id(1)
    c_len = GDN_CHUNK
    n_chunks = n_tok // c_len
    n_sub = n_tok // SUB
    grp = GDN_GROUP if n_chunks % GDN_GROUP == 0 else SUB // c_len
    halo = 8

    pad_ref[0:halo, :] = jnp.zeros((halo, HD), F32)
    pad_ref[halo + n_tok:halo + n_tok + halo, :] = jnp.zeros((halo, HD), F32)
    win_row = lax.broadcasted_iota(jnp.int32, (SUB + 2 * halo, HD), 0)

    for src_ref, cw_ref, dst_ref, norm_scale in ((q_ref, cwq_ref, q32_ref, HD ** -0.5),
                                                 (k_ref, cwk_ref, k32_ref, 1.0),
                                                 (v_ref, cwv_ref, v32_ref, None)):
        def fill(s, _):
            rows = pl.ds(pl.multiple_of(s * SUB, SUB), SUB)
            pad_ref[pl.ds(pl.multiple_of(halo + s * SUB, 8), SUB), :] = src_ref[0, rows, :].astype(F32)
            return 0

        lax.fori_loop(0, n_sub, fill, 0)
        cw = cw_ref[...]

        def conv(s, _):
            win = pad_ref[pl.ds(pl.multiple_of(s * SUB, SUB), SUB + 2 * halo), :]
            tok = win_row + (s * SUB - halo)
            same = (tok >= SUB).astype(jnp.int32) == jnp.minimum(s, 1)
            win = jnp.where(same, win, 0.0)
            y = jnp.zeros((SUB, HD), F32)
            for j in range(GDN_CONV):
                off = halo + j - GDN_CONV // 2
                y = y + win[off:off + SUB, :] * cw[j:j + 1, :]
            y = _silu(y)
            if norm_scale is not None:
                y = y * (lax.rsqrt(jnp.sum(y * y, axis=-1, keepdims=True) + EPS) * norm_scale)
            dst_ref[pl.ds(pl.multiple_of(s * SUB, SUB), SUB), :] = y
            return 0

        lax.fori_loop(0, n_sub, conv, 0)

    ri = lax.broadcasted_iota(jnp.int32, (grp, c_len, c_len), 1)
    ci = lax.broadcasted_iota(jnp.int32, (grp, c_len, c_len), 2)
    eye = (ri == ci).astype(F32)
    tri = (ri >= ci, ri <= ci)
    strict = (ri > ci, ri < ci)
    merge_masks = []
    m = 1
    while m < c_len:
        merge_masks.append(((ri // (2 * m)) == (ci // (2 * m))) & ((ri // m) != (ci // m)))
        m *= 2

    def bmm(a, b):
        return lax.dot_general(a, b, (((2,), (1,)), ((0,), (0,))), preferred_element_type=F32)

    def bmm_nt(a, b):
        return lax.dot_general(a, b, (((2,), (2,)), ((0,), (0,))), preferred_element_type=F32)

    def bmm_tn(a, b):
        return lax.dot_general(a, b, (((1,), (1,)), ((0,), (0,))), preferred_element_type=F32)

    def local(gi, _):
        r0 = pl.multiple_of(gi * (grp * c_len), grp * c_len)
        rows = pl.ds(r0, grp * c_len)
        q = q32_ref[rows, :].reshape(grp, c_len, HD)
        k = k32_ref[rows, :].reshape(grp, c_len, HD)
        v = v32_ref[rows, :].reshape(grp, c_len, HD)
        kb16 = k.astype(BF16)
        qk = bmm_nt(q.astype(BF16), kb16)
        cols = col_ref[0, rows, :]
        col_lane = lax.broadcasted_iota(jnp.int32, cols.shape, 1)

        def pick(lane0):
            sel = jnp.where(col_lane == lane0 + head, cols, 0.0)
            return jnp.sum(sel, axis=-1, keepdims=True).reshape(grp, c_len, 1)

        for d in range(2):
            gc = pick(LANE_GC + GDN_HEADS * d)
            beta = pick(LANE_BETA + GDN_HEADS * d)
            gc_row = jnp.stack([row_ref[0, d, 0, pl.ds(gi * grp + g, 1), :] for g in range(grp)], axis=0)
            g_tot = gc[:, c_len - 1:c_len, :] if d == 0 else gc[:, 0:1, :]
            decay = jnp.exp(jnp.where(tri[d], gc - gc_row, -jnp.inf))
            kbeta = k * beta
            a = jnp.where(strict[d], bmm_nt(kbeta.astype(BF16), kb16) * decay, 0.0)
            t = eye - jnp.where(merge_masks[0], a, 0.0)
            for mm in merge_masks[1:]:
                t16 = t.astype(BF16)
                t = t - bmm(bmm(t16, jnp.where(mm, a, 0.0).astype(BF16)).astype(BF16), t16)
            rhs = jnp.concatenate([v * beta, kbeta * jnp.exp(gc)], axis=-1).astype(BF16)
            sol = bmm(t.astype(BF16), rhs)
            u, w16 = sol[:, :, :HD], sol[:, :, HD:].astype(BF16)
            u_ref[d, rows, :] = u.reshape(grp * c_len, HD)
            w_ref[d, rows, :] = w16.reshape(grp * c_len, HD)
            qe_ref[d, rows, :] = (q * jnp.exp(gc)).reshape(grp * c_len, HD).astype(BF16)
            att_ref[d, rows, :] = (qk * decay).reshape(grp * c_len, c_len).astype(BF16)
            kd = (k * jnp.exp(g_tot - gc)).astype(BF16)
            srows = pl.ds(pl.multiple_of(gi * (grp * HD), grp * HD), grp * HD)
            sq_ref[d, srows, :] = bmm_tn(kd, u.astype(BF16)).reshape(grp * HD, HD).astype(BF16)
            sm_ref[d, srows, :] = bmm_tn(kd, w16).reshape(grp * HD, HD).astype(BF16)
            eg_ref[d, pl.ds(pl.multiple_of(gi * grp, grp), grp), :] = jnp.broadcast_to(
                jnp.exp(g_tot).reshape(grp, 1), (grp, HD))
        return 0

    lax.fori_loop(0, n_chunks // grp, local, 0)

    ctx_chunks = SUB // c_len

    def step(d, c, s):
        rows = pl.ds(pl.multiple_of(c * c_len, c_len), c_len)
        srows = pl.ds(pl.multiple_of(c * HD, HD), HD)
        s16 = s.astype(BF16)
        v_new = (u_ref[d, rows, :] - _dot(w_ref[d, rows, :], s16)).astype(BF16)
        acc_ref[d, rows, :] = _dot(qe_ref[d, rows, :], s16) + _dot(att_ref[d, rows, :], v_new)
        return s * eg_ref[d, pl.ds(c, 1), :] + sq_ref[d, srows, :].astype(F32) - _dot(sm_ref[d, srows, :], s16)

    s0 = jnp.zeros((HD, HD), F32)
    scan_unroll = 4

    def scan(i, carry):
        s_f, s_b = carry
        for r in range(scan_unroll):
            j = i * scan_unroll + r
            s_f = step(0, j, s_f)
            s_b = step(1, jnp.where(j < ctx_chunks, ctx_chunks - 1 - j, n_chunks - 1 + ctx_chunks - j), s_b)
        return s_f, s_b

    lax.fori_loop(0, n_chunks // scan_unroll, scan, (s0, s0))

    def finish(s, _):
        rows = pl.ds(pl.multiple_of(s * SUB, SUB), SUB)
        o = acc_ref[0, rows, :] + acc_ref[1, rows, :]
        y = o * lax.rsqrt(jnp.mean(o * o, axis=-1, keepdims=True) + EPS) * nw_ref[...]
        o_ref[0, rows, :] = (_silu(g_ref[0, rows, :].astype(F32)) * y).astype(BF16)
        return 0

    lax.fori_loop(0, n_sub, finish, 0)


def _gdn(z3, conv_w, cols, rows, norm_w):
    b, t, _ = z3.shape
    cb = COL_GDN // HD
    n_chunks = t // GDN_CHUNK

    once = pl.Buffered(1)

    def col(off):
        return pl.BlockSpec((1, t, HD), lambda i, h: (i, 0, cb + off + h), pipeline_mode=once)

    def cw(off):
        return pl.BlockSpec((GDN_CONV, HD), lambda i, h: (0, off + h))

    return pl.pallas_call(
        functools.partial(_gdn_kernel, n_tok=t),
        grid=(b, GDN_HEADS),
        in_specs=[col(0), col(4), col(8), col(12), cw(0), cw(4), cw(8),
                  pl.BlockSpec((1, t, NZS), lambda i, h: (i, 0, 0), pipeline_mode=once),
                  pl.BlockSpec((1, 2, 1, n_chunks, GDN_CHUNK), lambda i, h: (i, 0, h, 0, 0)),
                  pl.BlockSpec((1, HD), lambda i, h: (0, 0))],
        out_specs=pl.BlockSpec((1, t, HD), lambda i, h: (i, 0, h)),
        out_shape=jax.ShapeDtypeStruct((b, t, GDN_HEADS * HD), BF16),
        scratch_shapes=[pltpu.VMEM((t + 16, HD), F32),
                        pltpu.VMEM((t, HD), F32), pltpu.VMEM((t, HD), F32), pltpu.VMEM((t, HD), F32),
                        pltpu.VMEM((2, t, HD), BF16), pltpu.VMEM((2, t, HD), F32),
                        pltpu.VMEM((2, t, HD), BF16),
                        pltpu.VMEM((2, n_chunks * HD, HD), BF16), pltpu.VMEM((2, n_chunks * HD, HD), BF16),
                        pltpu.VMEM((2, t, GDN_CHUNK), BF16),
                        pltpu.VMEM((2, n_chunks, HD), F32),
                        pltpu.VMEM((2, t, HD), F32)],
        compiler_params=_cparams(("arbitrary", "arbitrary")),
        name="gdn",
    )(z3, z3, z3, z3, conv_w, conv_w, conv_w, cols, rows, norm_w.reshape(1, HD))


def _mla_prep_kernel(cq_ref, ckv_ref, zs_ref, qnw_ref, kvnw_ref, wq_ref, wkv_ref, cos_ref, sin_ref,
                     q_ref, k_ref, v_ref):
    def rms(x, w):
        return (x * lax.rsqrt(jnp.mean(x * x, axis=-1, keepdims=True) + EPS) * w).astype(BF16)

    cos, sin = cos_ref[...], sin_ref[...]

    def rope(x):
        return x * cos + _swap_halves(x, 16) * sin

    q = _dot(rms(cq_ref[...].astype(F32), qnw_ref[...]), wq_ref[...])
    kv = _dot(rms(ckv_ref[...].astype(F32), kvnw_ref[...]), wkv_ref[...])
    lane = lax.broadcasted_iota(jnp.int32, (SUB, HD), 1)
    kr = rope(jnp.where(lane < MLA_ROPE, zs_ref[...], 0.0)).astype(BF16)
    for h in range(MLA_HEADS):
        base = h * MLA_QK
        q_ref[0, h, :, 0:HD] = (q[:, base:base + HD] * MLA_Q_SCALE).astype(BF16)
        q_ref[0, h, :, HD:MLA_QK] = (rope(q[:, base + HD:base + MLA_QK]) * MLA_Q_SCALE).astype(BF16)
        k_ref[0, h, :, 0:HD] = kv[:, h * HD:(h + 1) * HD].astype(BF16)
        k_ref[0, h, :, HD:MLA_QK] = kr
        v_ref[0, h] = kv[:, (MLA_HEADS + h) * HD:(MLA_HEADS + h + 1) * HD].astype(BF16)


def _mla_prep(z, zs, qnw, kvnw, wq, wkv, cos, sin, b, t):
    spb = t // SUB

    def tok(width, blk):
        return pl.BlockSpec((SUB, width), lambda i: (i, blk))

    def par(shape):
        return pl.BlockSpec(shape, lambda i: (0, 0))

    def head_out(width):
        return pl.BlockSpec((1, MLA_HEADS, SUB, width), lambda i: (i // spb, 0, i % spb, 0))

    return pl.pallas_call(
        _mla_prep_kernel,
        grid=(b * spb,),
        in_specs=[tok(MLA_Q_RANK, COL_CQ // MLA_Q_RANK), tok(MLA_KV_RANK, COL_CKV // MLA_KV_RANK), tok(NZS, 0),
                  par((1, MLA_Q_RANK)), par((1, MLA_KV_RANK)),
                  par(wq.shape), par(wkv.shape),
                  pl.BlockSpec((SUB, HD), lambda i: (i % spb, 0)),
                  pl.BlockSpec((SUB, HD), lambda i: (i % spb, 0))],
        out_specs=[head_out(MLA_QK), head_out(MLA_QK), head_out(HD)],
        out_shape=[jax.ShapeDtypeStruct((b, MLA_HEADS, t, MLA_QK), BF16),
                   jax.ShapeDtypeStruct((b, MLA_HEADS, t, MLA_QK), BF16),
                   jax.ShapeDtypeStruct((b, MLA_HEADS, t, HD), BF16)],
        compiler_params=_cparams(("arbitrary",)),
        name="mla_prep",
    )(z, z, zs, qnw.reshape(1, -1), kvnw.reshape(1, -1), wq, wkv, cos, sin)


def _mla_attn_kernel(q_ref, k_ref, v_ref, g_ref, o_ref):
    q = q_ref[0, 0]
    gate = _silu(g_ref[0].astype(F32))

    def attend(n_keys):
        s = _dot_nt(q, k_ref[0, 0, 0:n_keys, :])
        p = jnp.exp2(s - jnp.max(s, axis=-1, keepdims=True))
        l = jnp.sum(p, axis=-1, keepdims=True)
        o = _dot(p.astype(BF16), v_ref[0, 0, 0:n_keys, :]) / l
        o_ref[0] = (gate * o).astype(BF16)

    @pl.when(pl.program_id(2) == 0)
    def _():
        attend(SUB)

    @pl.when(pl.program_id(2) > 0)
    def _():
        attend(k_ref.shape[2])


def _mla_attn(qp, kp, vp, z3):
    b, nh, t, _ = qp.shape
    gb = COL_MGATE // HD
    return pl.pallas_call(
        _mla_attn_kernel,
        grid=(b, nh, t // SUB),
        in_specs=[pl.BlockSpec((1, 1, SUB, MLA_QK), lambda i, h, j: (i, h, j, 0)),
                  pl.BlockSpec((1, 1, t, MLA_QK), lambda i, h, j: (i, h, 0, 0)),
                  pl.BlockSpec((1, 1, t, HD), lambda i, h, j: (i, h, 0, 0)),
                  pl.BlockSpec((1, SUB, HD), lambda i, h, j: (i, j, gb + h))],
        out_specs=pl.BlockSpec((1, SUB, HD), lambda i, h, j: (i, j, h)),
        out_shape=jax.ShapeDtypeStruct((b, t, nh * HD), BF16),
        compiler_params=_cparams(("arbitrary", "arbitrary", "arbitrary")),
        name="mla_attn",
    )(qp, kp, vp, z3)


def _layout_w_in(w_in):
    sizes = (512, 512, 512, 512, 512, 512, 512, 512, 8, 8, MLA_Q_RANK, MLA_KV_RANK, MLA_ROPE, 1024)
    offs = np.concatenate([[0], np.cumsum(sizes)])
    part = lambda n: w_in[:, offs[n]:offs[n + 1]]
    w_main = jnp.concatenate([part(n) for n in (0, 1, 2, 3, 4, 5, 6, 7, 11, 10, 13)], axis=-1)
    w_small = jnp.concatenate([part(12), part(8), part(9)], axis=-1)
    w_small = jnp.pad(w_small, ((0, 0), (0, NZS - w_small.shape[-1])))
    return w_main.astype(BF16), w_small.astype(BF16)


def _layout_w_qb(w_qb):
    w = w_qb.reshape(MLA_Q_RANK, MLA_HEADS, HD + MLA_ROPE)
    w = jnp.pad(w, ((0, 0), (0, 0), (0, MLA_QK - HD - MLA_ROPE)))
    return w.reshape(MLA_Q_RANK, MLA_HEADS * MLA_QK).astype(BF16)


def _layout_w_kvb(w_kvb):
    w = w_kvb.reshape(MLA_KV_RANK, MLA_HEADS, 2, HD)
    return jnp.transpose(w, (0, 2, 1, 3)).reshape(MLA_KV_RANK, 2 * MLA_HEADS * HD).astype(BF16)


def kernel(x, c, ctx, c_ctx, ada_w, ada_b, norm_w, w_in, ret_log_decay, ret_norm_w, gdn_conv_w, gdn_a_log,
           gdn_dt_bias, gdn_norm_w, mla_q_norm_w, mla_w_qb, mla_kv_norm_w, mla_w_kvb, w_out, final_norm_w):
    b, seq, d = x.shape
    n_ctx = ctx.shape[1]
    assert n_ctx == SUB and seq % SUB == 0 and d == D_MODEL
    depth = ada_w.shape[0]
    t = n_ctx + seq
    spb = t // SUB
    n_chunks = t // GDN_CHUNK

    cc = jnp.zeros((16, d), F32).at[:b].set(c).at[b].set(c_ctx)
    mod = _adaln(cc, ada_w, ada_b)
    sel = np.stack([np.full(b, b), np.arange(b)], axis=1).reshape(-1)

    ret_cos, ret_sin = _rope_tables(n_ctx, seq, HD)
    mla_cos, mla_sin = _rope_tables(n_ctx, seq, MLA_ROPE)

    xa = jnp.concatenate([ctx, x], axis=1).reshape(b * t, d)
    for i in range(depth):
        final = i == depth - 1
        shift, scale, gate = mod[i, :, :d], mod[i, :, d:2 * d], mod[i, :, 2 * d:]
        ms = jnp.stack([scale[sel], shift[sel]], axis=1)
        gate_tab = gate[sel][:, None, :]
        w_main, w_small = _layout_w_in(w_in[i])

        z, zs = _inproj(xa, ms, norm_w[i], w_main, w_small, spb)
        z3 = z.reshape(b, t, NZ)

        m_ret = _retention(z3, ret_log_decay[i], ret_norm_w[i], ret_cos, ret_sin)

        on_lanes = lambda p: jnp.zeros((1, NZS), F32).at[0, LANE_GC:LANE_GC + 2 * GDN_HEADS].set(p.reshape(-1))
        cols, rows_t = _gdn_gates(zs, on_lanes(gdn_a_log[i]), on_lanes(gdn_dt_bias[i]), b, t)
        gc_rows = rows_t[:, LANE_GC:LANE_GC + 2 * GDN_HEADS].reshape(b, 2, GDN_HEADS, n_chunks, GDN_CHUNK)
        m_gdn = _gdn(z3, gdn_conv_w[i], cols.reshape(b, t, NZS), gc_rows, gdn_norm_w[i])

        qp, kp, vp = _mla_prep(z, zs, mla_q_norm_w[i], mla_kv_norm_w[i], _layout_w_qb(mla_w_qb[i]),
                               _layout_w_kvb(mla_w_kvb[i]), mla_cos, mla_sin, b, t)
        m_mla = _mla_attn(qp, kp, vp, z3)

        xa = _outproj(xa, m_ret.reshape(b * t, -1), m_gdn.reshape(b * t, -1), m_mla.reshape(b * t, -1),
                      gate_tab, w_out[i].astype(BF16), final_norm_w, spb, final)
    return xa.reshape(b, seq, d)
```

```python
import functools
import math

import jax
import jax.numpy as jnp
import numpy as np
from jax import lax
from jax.experimental import pallas as pl
from jax.experimental.pallas import tpu as pltpu

F32 = jnp.float32
BF16 = jnp.bfloat16

EPS = 1e-6
ROPE_BASE = 10000.0
GRID_W = 64

D_MODEL = 2048
SUB = 256
RET_HEADS = 4
GDN_HEADS = 4
HD = 128
GDN_CONV = 5
GDN_CHUNK = 64
MLA_HEADS = 8
MLA_ROPE = 64
MLA_Q_RANK = 768
MLA_KV_RANK = 512
MLA_QK = 256
MLA_SCALE = (HD + MLA_ROPE) ** -0.5
MLA_Q_SCALE = MLA_SCALE * math.log2(math.e)

COL_RET = 0
COL_GDN = 2048
COL_CKV = 4096
COL_CQ = 4608
COL_MGATE = 5376
NZ = 6400
NZS = 128

VMEM_LIMIT = 56 * 1024 * 1024


def _cparams(sem):
    return pltpu.CompilerParams(dimension_semantics=sem, vmem_limit_bytes=VMEM_LIMIT)


def _silu(x):
    return x * (1.0 / (1.0 + jnp.exp(-x)))


def _dot(a, b):
    return jnp.dot(a, b, preferred_element_type=F32)


def _dot_nt(a, b):
    return lax.dot_general(a, b, (((1,), (1,)), ((), ())), preferred_element_type=F32)


def _dot_tn(a, b):
    return lax.dot_general(a, b, (((0,), (0,)), ((), ())), preferred_element_type=F32)


def _adaln_kernel(c_ref, w_ref, b_ref, o_ref):
    s = _silu(c_ref[...]).astype(BF16)
    o_ref[0] = _dot(s, w_ref[0].astype(BF16)) + b_ref[0]


def _adaln(cc, ada_w, ada_b):
    depth, d, n3 = ada_w.shape
    tn = 768
    return pl.pallas_call(
        _adaln_kernel,
        grid=(depth, n3 // tn),
        in_specs=[pl.BlockSpec((16, d), lambda i, j: (0, 0)),
                  pl.BlockSpec((1, d, tn), lambda i, j: (i, 0, j)),
                  pl.BlockSpec((1, 1, tn), lambda i, j: (i, 0, j))],
        out_specs=pl.BlockSpec((1, 16, tn), lambda i, j: (i, 0, j)),
        out_shape=jax.ShapeDtypeStruct((depth, 16, n3), F32),
        compiler_params=_cparams(("arbitrary", "arbitrary")),
        name="adaln",
    )(cc, ada_w, ada_b.reshape(depth, 1, n3))


def _mod_row(sub_idx, subs_per_batch):
    b = sub_idx // subs_per_batch
    is_lat = (sub_idx % subs_per_batch != 0).astype(jnp.int32)
    return b * 2 + is_lat


def _inproj_kernel(x_ref, ms_ref, nw_ref, w_ref, ws_ref, z_ref, zs_ref, h_ref, *, n_sub, subs_per_batch):
    i = pl.program_id(0)

    @pl.when(pl.program_id(1) == 0)
    def _():
        for u in range(n_sub):
            m = ms_ref[_mod_row(i * n_sub + u, subs_per_batch)]
            x = x_ref[u * SUB:(u + 1) * SUB, :]
            y = x * lax.rsqrt(jnp.mean(x * x, axis=-1, keepdims=True) + EPS) * nw_ref[...]
            h_ref[u * SUB:(u + 1) * SUB, :] = (y * (1.0 + m[0:1]) + m[1:2]).astype(BF16)
        zs_ref[...] = _dot(h_ref[...], ws_ref[...])

    z_ref[...] = _dot(h_ref[...], w_ref[...]).astype(BF16)


def _inproj(xa, ms, norm_w, w_main, w_small, subs_per_batch):
    bt, d = xa.shape
    tm = 1024 if bt % 1024 == 0 else SUB
    tn = 1280
    return pl.pallas_call(
        functools.partial(_inproj_kernel, n_sub=tm // SUB, subs_per_batch=subs_per_batch),
        grid=(bt // tm, NZ // tn),
        in_specs=[pl.BlockSpec((tm, d), lambda i, j: (i, 0)),
                  pl.BlockSpec(ms.shape, lambda i, j: (0, 0, 0)),
                  pl.BlockSpec((1, d), lambda i, j: (0, 0)),
                  pl.BlockSpec((d, tn), lambda i, j: (0, j)),
                  pl.BlockSpec((d, NZS), lambda i, j: (0, 0))],
        out_specs=[pl.BlockSpec((tm, tn), lambda i, j: (i, j)),
                   pl.BlockSpec((tm, NZS), lambda i, j: (i, 0))],
        out_shape=[jax.ShapeDtypeStruct((bt, NZ), BF16),
                   jax.ShapeDtypeStruct((bt, NZS), F32)],
        scratch_shapes=[pltpu.VMEM((tm, d), BF16)],
        compiler_params=_cparams(("arbitrary", "arbitrary")),
        name="inproj",
    )(xa, ms, norm_w.reshape(1, d), w_main, w_small)


def _outproj_kernel(x_ref, r_ref, g_ref, m_ref, gate_ref, w_ref, fw_ref, o_ref, *, n_sub, subs_per_batch, final):
    i = pl.program_id(0)
    acc = (_dot(r_ref[...], w_ref[0:512, :]) + _dot(g_ref[...], w_ref[512:1024, :])
           + _dot(m_ref[...], w_ref[1024:2048, :]))
    for u in range(n_sub):
        row = (i // (subs_per_batch - 1)) * 2 + 1 if final else _mod_row(i * n_sub + u, subs_per_batch)
        gate = gate_ref[row]
        rows = slice(u * SUB, (u + 1) * SUB)
        xn = x_ref[rows, :] + gate * acc[rows, :]
        if final:
            xn = xn * lax.rsqrt(jnp.mean(xn * xn, axis=-1, keepdims=True) + EPS) * fw_ref[...]
        o_ref[rows, :] = xn


def _outproj(xa, mret, mgdn, mmla, gate_tab, w_out, final_w, subs_per_batch, final):
    bt, d = xa.shape
    if final:
        tm, lat = SUB, subs_per_batch - 1
        n_blocks = (bt // SUB // subs_per_batch) * lat
        src = lambda i: ((i // lat) * subs_per_batch + 1 + i % lat, 0)
        aliases = {}
    else:
        tm = 512 if bt % 512 == 0 else SUB
        n_blocks = bt // tm
        src = lambda i: (i, 0)
        aliases = {0: 0}
    return pl.pallas_call(
        functools.partial(_outproj_kernel, n_sub=tm // SUB, subs_per_batch=subs_per_batch, final=final),
        grid=(n_blocks,),
        in_specs=[pl.BlockSpec((tm, d), src),
                  pl.BlockSpec((tm, 512), src),
                  pl.BlockSpec((tm, 512), src),
                  pl.BlockSpec((tm, 1024), src),
                  pl.BlockSpec(gate_tab.shape, lambda i: (0, 0, 0)),
                  pl.BlockSpec((d, d), lambda i: (0, 0)),
                  pl.BlockSpec((1, d), lambda i: (0, 0))],
        out_specs=pl.BlockSpec((tm, d), lambda i: (i, 0)),
        out_shape=jax.ShapeDtypeStruct((n_blocks * tm, d), F32),
        input_output_aliases=aliases,
        compiler_params=_cparams(("arbitrary",)),
        name="outproj",
    )(xa, mret, mgdn, mmla, gate_tab, w_out, final_w.reshape(1, d))


def _swap_halves(x, half):
    lane = lax.broadcasted_iota(jnp.int32, x.shape, x.ndim - 1)
    first = (lane % (2 * half)) < half
    return jnp.where(first, pltpu.roll(x, 128 - half, x.ndim - 1), pltpu.roll(x, half, x.ndim - 1))


def _rope_tables(n_ctx, seq, dim):
    rows = seq // GRID_W
    row = jnp.repeat(jnp.arange(rows), GRID_W).astype(F32)
    col = jnp.tile(jnp.arange(GRID_W), rows).astype(F32)
    m = dim // 2
    inv = ROPE_BASE ** (-jnp.arange(m // 2, dtype=F32) / (m // 2))
    ar, ac = row[:, None] * inv, col[:, None] * inv
    cos = jnp.concatenate([jnp.cos(ar), jnp.cos(ar), jnp.cos(ac), jnp.cos(ac)], axis=-1)
    sin = jnp.concatenate([-jnp.sin(ar), jnp.sin(ar), -jnp.sin(ac), jnp.sin(ac)], axis=-1)
    cos = jnp.pad(cos, ((n_ctx, 0), (0, 128 - dim)), constant_values=1.0)
    sin = jnp.pad(sin, ((n_ctx, 0), (0, 128 - dim)))
    return cos, sin


def _ret_kernel(dec_ref, q_ref, k_ref, v_ref, g_ref, cos_ref, sin_ref, nw_ref, o_ref,
                acc_ref, qd_ref, inc_ref, st_ref, *, n_chunks):
    h = pl.program_id(1)
    gf = dec_ref[0, h]
    gb = dec_ref[1, h]
    c_len = SUB

    ri = lax.broadcasted_iota(jnp.int32, (c_len, c_len), 0)
    ci = lax.broadcasted_iota(jnp.int32, (c_len, c_len), 1)
    dist = (ri - ci).astype(F32)
    mask = (jnp.where(ri >= ci, jnp.exp(gf * jnp.maximum(dist, 0.0)), 0.0)
            + jnp.where(ri <= ci, jnp.exp(gb * jnp.maximum(-dist, 0.0)), 0.0))
    pos = lax.broadcasted_iota(jnp.int32, (c_len, HD), 0).astype(F32)
    q_dec_f = jnp.exp(gf * (pos + 1.0))
    q_dec_b = jnp.exp(gb * (c_len - pos))
    k_dec_f = jnp.exp(gf * (c_len - 1.0 - pos))
    k_dec_b = jnp.exp(gb * pos)
    s_dec_f = jnp.exp(gf * c_len)
    s_dec_b = jnp.exp(gb * c_len)

    def rows_of(c, n=c_len):
        return pl.ds(c * n if isinstance(c, int) else pl.multiple_of(c * n, n), n)

    grp = math.gcd(n_chunks - 1, 4)

    def for_chunks(fn):
        fn(0)

        def body(g, _):
            for r in range(grp):
                fn(1 + g * grp + r)
            return 0

        lax.fori_loop(0, (n_chunks - 1) // grp, body, 0)

    def local(c):
        rows = rows_of(c)
        cos, sin = cos_ref[rows, :], sin_ref[rows, :]
        q = q_ref[0, rows, :].astype(F32)
        k = k_ref[0, rows, :].astype(F32)
        q = q * cos + _swap_halves(q, 32) * sin
        k = (k * cos + _swap_halves(k, 32) * sin) * (HD ** -0.5)
        v = v_ref[0, rows, :]
        p = (_dot_nt(q.astype(BF16), k.astype(BF16)) * mask).astype(BF16)
        acc_ref[rows, :] = _dot(p, v)
        qd_ref[0, rows, :] = (q * q_dec_f).astype(BF16)
        qd_ref[1, rows, :] = (q * q_dec_b).astype(BF16)
        inc_ref[0, rows_of(c, HD), :] = _dot_tn((k * k_dec_f).astype(BF16), v)
        inc_ref[1, rows_of(c, HD), :] = _dot_tn((k * k_dec_b).astype(BF16), v)

    for_chunks(local)

    def scan(i, carry):
        s_f, s_b = carry
        cb = jnp.where(i == 0, 0, n_chunks - i)
        st_ref[0, rows_of(i, HD), :] = s_f.astype(BF16)
        st_ref[1, rows_of(cb, HD), :] = s_b.astype(BF16)
        return (s_f * s_dec_f + inc_ref[0, rows_of(i, HD), :], s_b * s_dec_b + inc_ref[1, rows_of(cb, HD), :])

    lax.fori_loop(0, n_chunks, scan, (jnp.zeros((HD, HD), F32), jnp.zeros((HD, HD), F32)))

    def finish(c):
        rows = rows_of(c)
        o = (acc_ref[rows, :] + _dot(qd_ref[0, rows, :], st_ref[0, rows_of(c, HD), :])
             + _dot(qd_ref[1, rows, :], st_ref[1, rows_of(c, HD), :]))
        y = o * lax.rsqrt(jnp.mean(o * o, axis=-1, keepdims=True) + EPS) * nw_ref[...]
        o_ref[0, rows, :] = (_silu(g_ref[0, rows, :].astype(F32)) * y).astype(BF16)

    for_chunks(finish)


def _retention(z3, log_decay, norm_w, cos, sin):
    b, t, _ = z3.shape
    cb = COL_RET // HD

    def col(off):
        return pl.BlockSpec((1, t, HD), lambda i, h: (i, 0, cb + off + h))

    return pl.pallas_call(
        functools.partial(_ret_kernel, n_chunks=t // SUB),
        grid=(b, RET_HEADS),
        in_specs=[pl.BlockSpec(memory_space=pltpu.SMEM),
                  col(0), col(4), col(8), col(12),
                  pl.BlockSpec((t, HD), lambda i, h: (0, 0)),
                  pl.BlockSpec((t, HD), lambda i, h: (0, 0)),
                  pl.BlockSpec((1, HD), lambda i, h: (0, 0))],
        out_specs=pl.BlockSpec((1, t, HD), lambda i, h: (i, 0, h)),
        out_shape=jax.ShapeDtypeStruct((b, t, RET_HEADS * HD), BF16),
        scratch_shapes=[pltpu.VMEM((t, HD), F32), pltpu.VMEM((2, t, HD), BF16),
                        pltpu.VMEM((2, (t // SUB) * HD, HD), F32), pltpu.VMEM((2, (t // SUB) * HD, HD), BF16)],
        compiler_params=_cparams(("arbitrary", "arbitrary")),
        name="retention",
    )(log_decay, z3, z3, z3, z3, cos, sin, norm_w.reshape(1, HD))


LANE_GC = 64
LANE_BETA = 72


def _gdn_gate_kernel(zs_ref, alog_ref, dtb_ref, col_ref, row_ref):
    x = zs_ref[...]
    z = x + dtb_ref[...]
    softplus = jnp.maximum(z, 0.0) + jnp.log(1.0 + jnp.exp(-jnp.abs(z)))
    g = -jnp.exp(alog_ref[...]) * softplus
    c = GDN_CHUNK
    ri = lax.broadcasted_iota(jnp.int32, (c, c), 0)
    ci = lax.broadcasted_iota(jnp.int32, (c, c), 1)
    lower, upper = (ri >= ci).astype(F32), (ri <= ci).astype(F32)
    lane = lax.broadcasted_iota(jnp.int32, (c, NZS), 1)
    hi = lax.Precision.HIGHEST
    for j in range(SUB // c):
        gj = g[j * c:(j + 1) * c, :]
        pre = jnp.dot(lower, gj, precision=hi, preferred_element_type=F32)
        suf = jnp.dot(upper, gj, precision=hi, preferred_element_type=F32)
        beta = 1.0 / (1.0 + jnp.exp(-x[j * c:(j + 1) * c, :]))
        col_ref[j * c:(j + 1) * c, :] = jnp.where(lane < LANE_GC + GDN_HEADS, pre, jnp.where(lane < LANE_BETA, suf, beta))
    row_ref[0] = col_ref[...].T


def _gdn_gates(zs, alog_lane, dtb_lane, b, t):
    spb = t // SUB
    return pl.pallas_call(
        _gdn_gate_kernel,
        grid=(b * spb,),
        in_specs=[pl.BlockSpec((SUB, NZS), lambda i: (i, 0)),
                  pl.BlockSpec((1, NZS), lambda i: (0, 0)),
                  pl.BlockSpec((1, NZS), lambda i: (0, 0))],
        out_specs=[pl.BlockSpec((SUB, NZS), lambda i: (i, 0)),
                   pl.BlockSpec((1, NZS, SUB), lambda i: (i // spb, 0, i % spb))],
        out_shape=[jax.ShapeDtypeStruct((b * t, NZS), F32),
                   jax.ShapeDtypeStruct((b, NZS, t), F32)],
        compiler_params=_cparams(("arbitrary",)),
        name="gdn_gates",
    )(zs, alog_lane, dtb_lane)


GDN_GROUP = 17


def _gdn_kernel(q_ref, k_ref, v_ref, g_ref, cwq_ref, cwk_ref, cwv_ref, col_ref, row_ref, nw_ref, o_ref,
                pad_ref, q32_ref, k32_ref, v32_ref, w_ref, u_ref, qe_ref, sq_ref, sm_ref, att_ref, eg_ref, acc_ref,
                *, n_tok):
    head = pl.program_id(1)
    c_len = GDN_CHUNK
    n_chunks = n_tok // c_len
    n_sub = n_tok // SUB
    grp = GDN_GROUP if n_chunks % GDN_GROUP == 0 else SUB // c_len
    halo = 8

    def pad_row(s):
        return pl.multiple_of(halo + s * SUB + halo * jnp.minimum(s, 1), 8)

    for gap in (0, halo + SUB, 2 * halo + n_tok):
        pad_ref[gap:gap + halo, :] = jnp.zeros((halo, HD), F32)

    for src_ref, cw_ref, dst_ref, norm_scale in ((q_ref, cwq_ref, q32_ref, HD ** -0.5),
                                                 (k_ref, cwk_ref, k32_ref, 1.0),
                                                 (v_ref, cwv_ref, v32_ref, None)):
        def fill(s, _):
            rows = pl.ds(pl.multiple_of(s * SUB, SUB), SUB)
            pad_ref[pl.ds(pad_row(s), SUB), :] = src_ref[0, rows, :].astype(F32)
            return 0

        lax.fori_loop(0, n_sub, fill, 0)
        cw = cw_ref[...]

        def conv(s, _):
            win = pad_ref[pl.ds(pl.multiple_of(pad_row(s) - halo, 8), SUB + 2 * halo), :]
            y = jnp.zeros((SUB, HD), F32)
            for j in range(GDN_CONV):
                off = halo + j - GDN_CONV // 2
                y = y + win[off:off + SUB, :] * cw[j:j + 1, :]
            y = _silu(y)
            if norm_scale is not None:
                y = y * (lax.rsqrt(jnp.sum(y * y, axis=-1, keepdims=True) + EPS) * norm_scale)
            dst_ref[pl.ds(pl.multiple_of(s * SUB, SUB), SUB), :] = y
            return 0

        lax.fori_loop(0, n_sub, conv, 0)

    ri = lax.broadcasted_iota(jnp.int32, (grp, c_len, c_len), 1)
    ci = lax.broadcasted_iota(jnp.int32, (grp, c_len, c_len), 2)
    eye = (ri == ci).astype(F32)
    tri = (ri >= ci, ri <= ci)
    strict = (ri > ci, ri < ci)
    merge_masks = []
    m = 1
    while m < c_len:
        merge_masks.append(((ri // (2 * m)) == (ci // (2 * m))) & ((ri // m) != (ci // m)))
        m *= 2

    def bmm(a, b):
        return lax.dot_general(a, b, (((2,), (1,)), ((0,), (0,))), preferred_element_type=F32)

    def bmm_nt(a, b):
        return lax.dot_general(a, b, (((2,), (2,)), ((0,), (0,))), preferred_element_type=F32)

    def bmm_tn(a, b):
        return lax.dot_general(a, b, (((1,), (1,)), ((0,), (0,))), preferred_element_type=F32)

    def local(gi, _):
        r0 = pl.multiple_of(gi * (grp * c_len), grp * c_len)
        rows = pl.ds(r0, grp * c_len)
        q = q32_ref[rows, :].reshape(grp, c_len, HD)
        k = k32_ref[rows, :].reshape(grp, c_len, HD)
        v = v32_ref[rows, :].reshape(grp, c_len, HD)
        kb16 = k.astype(BF16)
        qk = bmm_nt(q.astype(BF16), kb16)
        cols = col_ref[0, rows, :]
        col_lane = lax.broadcasted_iota(jnp.int32, cols.shape, 1)

        def pick(lane0):
            sel = jnp.where(col_lane == lane0 + head, cols, 0.0)
            return jnp.sum(sel, axis=-1, keepdims=True).reshape(grp, c_len, 1)

        for d in range(2):
            gc = pick(LANE_GC + GDN_HEADS * d)
            beta = pick(LANE_BETA + GDN_HEADS * d)
            gc_row = jnp.stack([row_ref[0, d, 0, pl.ds(gi * grp + g, 1), :] for g in range(grp)], axis=0)
            g_tot = gc[:, c_len - 1:c_len, :] if d == 0 else gc[:, 0:1, :]
            decay = jnp.exp(jnp.where(tri[d], gc - gc_row, -jnp.inf))
            kbeta = k * beta
            a = jnp.where(strict[d], bmm_nt(kbeta.astype(BF16), kb16) * decay, 0.0)
            t = eye - jnp.where(merge_masks[0], a, 0.0)
            for mm in merge_masks[1:]:
                t16 = t.astype(BF16)
                t = t - bmm(bmm(t16, jnp.where(mm, a, 0.0).astype(BF16)).astype(BF16), t16)
            rhs = jnp.concatenate([v * beta, kbeta * jnp.exp(gc)], axis=-1).astype(BF16)
            sol = bmm(t.astype(BF16), rhs)
            u, w16 = sol[:, :, :HD], sol[:, :, HD:].astype(BF16)
            u_ref[d, rows, :] = u.reshape(grp * c_len, HD)
            w_ref[d, rows, :] = w16.reshape(grp * c_len, HD)
            qe_ref[d, rows, :] = (q * jnp.exp(gc)).reshape(grp * c_len, HD).astype(BF16)
            att_ref[d, rows, :] = (qk * decay).reshape(grp * c_len, c_len).astype(BF16)
            kd = (k * jnp.exp(g_tot - gc)).astype(BF16)
            srows = pl.ds(pl.multiple_of(gi * (grp * HD), grp * HD), grp * HD)
            sq_ref[d, srows, :] = bmm_tn(kd, u.astype(BF16)).reshape(grp * HD, HD).astype(BF16)
            sm_ref[d, srows, :] = bmm_tn(kd, w16).reshape(grp * HD, HD).astype(BF16)
            eg_ref[d, pl.ds(pl.multiple_of(gi * grp, grp), grp), :] = jnp.broadcast_to(
                jnp.exp(g_tot).reshape(grp, 1), (grp, HD))
        return 0

    lax.fori_loop(0, n_chunks // grp, local, 0)

    ctx_chunks = SUB // c_len

    def step(d, c, s):
        rows = pl.ds(pl.multiple_of(c * c_len, c_len), c_len)
        srows = pl.ds(pl.multiple_of(c * HD, HD), HD)
        s16 = s.astype(BF16)
        v_new = (u_ref[d, rows, :] - _dot(w_ref[d, rows, :], s16)).astype(BF16)
        acc_ref[d, rows, :] = _dot(qe_ref[d, rows, :], s16) + _dot(att_ref[d, rows, :], v_new)
        return s * eg_ref[d, pl.ds(c, 1), :] + sq_ref[d, srows, :].astype(F32) - _dot(sm_ref[d, srows, :], s16)

    s0 = jnp.zeros((HD, HD), F32)
    scan_unroll = 4

    def scan(i, carry):
        s_f, s_b = carry
        for r in range(scan_unroll):
            j = i * scan_unroll + r
            s_f = step(0, j, s_f)
            s_b = step(1, jnp.where(j < ctx_chunks, ctx_chunks - 1 - j, n_chunks - 1 + ctx_chunks - j), s_b)
        return s_f, s_b

    lax.fori_loop(0, n_chunks // scan_unroll, scan, (s0, s0))

    def finish(s, _):
        rows = pl.ds(pl.multiple_of(s * SUB, SUB), SUB)
        o = acc_ref[0, rows, :] + acc_ref[1, rows, :]
        y = o * lax.rsqrt(jnp.mean(o * o, axis=-1, keepdims=True) + EPS) * nw_ref[...]
        o_ref[0, rows, :] = (_silu(g_ref[0, rows, :].astype(F32)) * y).astype(BF16)
        return 0

    lax.fori_loop(0, n_sub, finish, 0)


def _gdn(z3, conv_w, cols, rows, norm_w):
    b, t, _ = z3.shape
    cb = COL_GDN // HD
    n_chunks = t // GDN_CHUNK

    once = pl.Buffered(1)

    def col(off):
        return pl.BlockSpec((1, t, HD), lambda i, h: (i, 0, cb + off + h), pipeline_mode=once)

    def cw(off):
        return pl.BlockSpec((GDN_CONV, HD), lambda i, h: (0, off + h))

    return pl.pallas_call(
        functools.partial(_gdn_kernel, n_tok=t),
        grid=(b, GDN_HEADS),
        in_specs=[col(0), col(4), col(8), col(12), cw(0), cw(4), cw(8),
                  pl.BlockSpec((1, t, NZS), lambda i, h: (i, 0, 0), pipeline_mode=once),
                  pl.BlockSpec((1, 2, 1, n_chunks, GDN_CHUNK), lambda i, h: (i, 0, h, 0, 0)),
                  pl.BlockSpec((1, HD), lambda i, h: (0, 0))],
        out_specs=pl.BlockSpec((1, t, HD), lambda i, h: (i, 0, h)),
        out_shape=jax.ShapeDtypeStruct((b, t, GDN_HEADS * HD), BF16),
        scratch_shapes=[pltpu.VMEM((t + 24, HD), F32),
                        pltpu.VMEM((t, HD), F32), pltpu.VMEM((t, HD), F32), pltpu.VMEM((t, HD), F32),
                        pltpu.VMEM((2, t, HD), BF16), pltpu.VMEM((2, t, HD), F32),
                        pltpu.VMEM((2, t, HD), BF16),
                        pltpu.VMEM((2, n_chunks * HD, HD), BF16), pltpu.VMEM((2, n_chunks * HD, HD), BF16),
                        pltpu.VMEM((2, t, GDN_CHUNK), BF16),
                        pltpu.VMEM((2, n_chunks, HD), F32),
                        pltpu.VMEM((2, t, HD), F32)],
        compiler_params=_cparams(("arbitrary", "arbitrary")),
        name="gdn",
    )(z3, z3, z3, z3, conv_w, conv_w, conv_w, cols, rows, norm_w.reshape(1, HD))


def _mla_prep_kernel(cq_ref, ckv_ref, zs_ref, qnw_ref, kvnw_ref, wq_ref, wkv_ref, cos_ref, sin_ref,
                     q_ref, k_ref, v_ref):
    def rms(x, w):
        return (x * lax.rsqrt(jnp.mean(x * x, axis=-1, keepdims=True) + EPS) * w).astype(BF16)

    cos, sin = cos_ref[...], sin_ref[...]

    def rope(x):
        return x * cos + _swap_halves(x, 16) * sin

    q = _dot(rms(cq_ref[...].astype(F32), qnw_ref[...]), wq_ref[...])
    kv = _dot(rms(ckv_ref[...].astype(F32), kvnw_ref[...]), wkv_ref[...])
    lane = lax.broadcasted_iota(jnp.int32, (SUB, HD), 1)
    kr = rope(jnp.where(lane < MLA_ROPE, zs_ref[...], 0.0)).astype(BF16)
    for h in range(MLA_HEADS):
        base = h * MLA_QK
        q_ref[0, h, :, 0:HD] = (q[:, base:base + HD] * MLA_Q_SCALE).astype(BF16)
        q_ref[0, h, :, HD:MLA_QK] = (rope(q[:, base + HD:base + MLA_QK]) * MLA_Q_SCALE).astype(BF16)
        k_ref[0, h, :, 0:HD] = kv[:, h * HD:(h + 1) * HD].astype(BF16)
        k_ref[0, h, :, HD:MLA_QK] = kr
        v_ref[0, h] = kv[:, (MLA_HEADS + h) * HD:(MLA_HEADS + h + 1) * HD].astype(BF16)


def _mla_prep(z, zs, qnw, kvnw, wq, wkv, cos, sin, b, t):
    spb = t // SUB

    def tok(width, blk):
        return pl.BlockSpec((SUB, width), lambda i: (i, blk))

    def par(shape):
        return pl.BlockSpec(shape, lambda i: (0, 0))

    def head_out(width):
        return pl.BlockSpec((1, MLA_HEADS, SUB, width), lambda i: (i // spb, 0, i % spb, 0))

    return pl.pallas_call(
        _mla_prep_kernel,
        grid=(b * spb,),
        in_specs=[tok(MLA_Q_RANK, COL_CQ // MLA_Q_RANK), tok(MLA_KV_RANK, COL_CKV // MLA_KV_RANK), tok(NZS, 0),
                  par((1, MLA_Q_RANK)), par((1, MLA_KV_RANK)),
                  par(wq.shape), par(wkv.shape),
                  pl.BlockSpec((SUB, HD), lambda i: (i % spb, 0)),
                  pl.BlockSpec((SUB, HD), lambda i: (i % spb, 0))],
        out_specs=[head_out(MLA_QK), head_out(MLA_QK), head_out(HD)],
        out_shape=[jax.ShapeDtypeStruct((b, MLA_HEADS, t, MLA_QK), BF16),
                   jax.ShapeDtypeStruct((b, MLA_HEADS, t, MLA_QK), BF16),
                   jax.ShapeDtypeStruct((b, MLA_HEADS, t, HD), BF16)],
        compiler_params=_cparams(("arbitrary",)),
        name="mla_prep",
    )(z, z, zs, qnw.reshape(1, -1), kvnw.reshape(1, -1), wq, wkv, cos, sin)


MLA_HEADS_PER_STEP = 4


def _mla_attn_kernel(q_ref, k_ref, v_ref, g0_ref, g1_ref, o_ref):
    def attend(n_keys):
        for hh in range(MLA_HEADS_PER_STEP):
            s = _dot_nt(q_ref[0, hh], k_ref[0, hh, 0:n_keys, :])
            p = jnp.exp2(s - jnp.max(s, axis=-1, keepdims=True))
            l = jnp.sum(p, axis=-1, keepdims=True)
            o = _dot(p.astype(BF16), v_ref[0, hh, 0:n_keys, :]) / l
            g_ref = (g0_ref, g1_ref)[hh // 2]
            gate = _silu(g_ref[0, :, (hh % 2) * HD:(hh % 2 + 1) * HD].astype(F32))
            o_ref[0, :, hh * HD:(hh + 1) * HD] = (gate * o).astype(BF16)

    @pl.when(pl.program_id(2) == 0)
    def _():
        attend(SUB)

    @pl.when(pl.program_id(2) > 0)
    def _():
        attend(k_ref.shape[2])


def _mla_attn(qp, kp, vp, z3):
    b, nh, t, _ = qp.shape
    hs = MLA_HEADS_PER_STEP
    assert hs == 4 and COL_MGATE % (2 * HD) == 0
    gb = COL_MGATE // (2 * HD)
    return pl.pallas_call(
        _mla_attn_kernel,
        grid=(b, nh // hs, t // SUB),
        in_specs=[pl.BlockSpec((1, hs, SUB, MLA_QK), lambda i, h, j: (i, h, j, 0)),
                  pl.BlockSpec((1, hs, t, MLA_QK), lambda i, h, j: (i, h, 0, 0)),
                  pl.BlockSpec((1, hs, t, HD), lambda i, h, j: (i, h, 0, 0)),
                  pl.BlockSpec((1, SUB, 2 * HD), lambda i, h, j: (i, j, gb + 2 * h)),
                  pl.BlockSpec((1, SUB, 2 * HD), lambda i, h, j: (i, j, gb + 2 * h + 1))],
        out_specs=pl.BlockSpec((1, SUB, hs * HD), lambda i, h, j: (i, j, h)),
        out_shape=jax.ShapeDtypeStruct((b, t, nh * HD), BF16),
        compiler_params=_cparams(("arbitrary", "arbitrary", "arbitrary")),
        name="mla_attn",
    )(qp, kp, vp, z3, z3)


def _layout_w_in(w_in):
    sizes = (512, 512, 512, 512, 512, 512, 512, 512, 8, 8, MLA_Q_RANK, MLA_KV_RANK, MLA_ROPE, 1024)
    offs = np.concatenate([[0], np.cumsum(sizes)])
    part = lambda n: w_in[:, offs[n]:offs[n + 1]]
    w_main = jnp.concatenate([part(n) for n in (0, 1, 2, 3, 4, 5, 6, 7, 11, 10, 13)], axis=-1)
    w_small = jnp.concatenate([part(12), part(8), part(9)], axis=-1)
    w_small = jnp.pad(w_small, ((0, 0), (0, NZS - w_small.shape[-1])))
    return w_main.astype(BF16), w_small.astype(BF16)


def _layout_w_qb(w_qb):
    w = w_qb.reshape(MLA_Q_RANK, MLA_HEADS, HD + MLA_ROPE)
    w = jnp.pad(w, ((0, 0), (0, 0), (0, MLA_QK - HD - MLA_ROPE)))
    return w.reshape(MLA_Q_RANK, MLA_HEADS * MLA_QK).astype(BF16)


def _layout_w_kvb(w_kvb):
    w = w_kvb.reshape(MLA_KV_RANK, MLA_HEADS, 2, HD)
    return jnp.transpose(w, (0, 2, 1, 3)).reshape(MLA_KV_RANK, 2 * MLA_HEADS * HD).astype(BF16)


def kernel(x, c, ctx, c_ctx, ada_w, ada_b, norm_w, w_in, ret_log_decay, ret_norm_w, gdn_conv_w, gdn_a_log,
           gdn_dt_bias, gdn_norm_w, mla_q_norm_w, mla_w_qb, mla_kv_norm_w, mla_w_kvb, w_out, final_norm_w):
    b, seq, d = x.shape
    n_ctx = ctx.shape[1]
    assert n_ctx == SUB and seq % SUB == 0 and d == D_MODEL
    depth = ada_w.shape[0]
    t = n_ctx + seq
    spb = t // SUB
    n_chunks = t // GDN_CHUNK

    cc = jnp.zeros((16, d), F32).at[:b].set(c).at[b].set(c_ctx)
    mod = _adaln(cc, ada_w, ada_b)
    sel = np.stack([np.full(b, b), np.arange(b)], axis=1).reshape(-1)

    ret_cos, ret_sin = _rope_tables(n_ctx, seq, HD)
    mla_cos, mla_sin = _rope_tables(n_ctx, seq, MLA_ROPE)

    xa = jnp.concatenate([ctx, x], axis=1).reshape(b * t, d)
    for i in range(depth):
        final = i == depth - 1
        shift, scale, gate = mod[i, :, :d], mod[i, :, d:2 * d], mod[i, :, 2 * d:]
        ms = jnp.stack([scale[sel], shift[sel]], axis=1)
        gate_tab = gate[sel][:, None, :]
        w_main, w_small = _layout_w_in(w_in[i])

        z, zs = _inproj(xa, ms, norm_w[i], w_main, w_small, spb)
        z3 = z.reshape(b, t, NZ)

        m_ret = _retention(z3, ret_log_decay[i], ret_norm_w[i], ret_cos, ret_sin)

        on_lanes = lambda p: jnp.zeros((1, NZS), F32).at[0, LANE_GC:LANE_GC + 2 * GDN_HEADS].set(p.reshape(-1))
        cols, rows_t = _gdn_gates(zs, on_lanes(gdn_a_log[i]), on_lanes(gdn_dt_bias[i]), b, t)
        gc_rows = rows_t[:, LANE_GC:LANE_GC + 2 * GDN_HEADS].reshape(b, 2, GDN_HEADS, n_chunks, GDN_CHUNK)
        m_gdn = _gdn(z3, gdn_conv_w[i], cols.reshape(b, t, NZS), gc_rows, gdn_norm_w[i])

        qp, kp, vp = _mla_prep(z, zs, mla_q_norm_w[i], mla_kv_norm_w[i], _layout_w_qb(mla_w_qb[i]),
                               _layout_w_kvb(mla_w_kvb[i]), mla_cos, mla_sin, b, t)
        m_mla = _mla_attn(qp, kp, vp, z3)

        xa = _outproj(xa, m_ret.reshape(b * t, -1), m_gdn.reshape(b * t, -1), m_mla.reshape(b * t, -1),
                      gate_tab, w_out[i].astype(BF16), final_norm_w, spb, final)
    return xa.reshape(b, seq, d)
```

```python
import functools
import math

import jax
import jax.numpy as jnp
import numpy as np
from jax import lax
from jax.experimental import pallas as pl
from jax.experimental.pallas import tpu as pltpu

F32 = jnp.float32
BF16 = jnp.bfloat16

EPS = 1e-6
ROPE_BASE = 10000.0
GRID_W = 64

D_MODEL = 2048
SUB = 256
RET_HEADS = 4
GDN_HEADS = 4
HD = 128
GDN_CONV = 5
GDN_CHUNK = 64
MLA_HEADS = 8
MLA_ROPE = 64
MLA_Q_RANK = 768
MLA_KV_RANK = 512
MLA_QK = 256
MLA_SCALE = (HD + MLA_ROPE) ** -0.5
MLA_Q_SCALE = MLA_SCALE * math.log2(math.e)

COL_RET = 0
COL_GDN = 2048
COL_CKV = 4096
COL_CQ = 4608
COL_MGATE = 5376
NZ = 6400
NZS = 128

VMEM_LIMIT = 56 * 1024 * 1024


def _cparams(sem):
    return pltpu.CompilerParams(dimension_semantics=sem, vmem_limit_bytes=VMEM_LIMIT)


def _silu(x):
    return x * (1.0 / (1.0 + jnp.exp(-x)))


def _dot(a, b):
    return jnp.dot(a, b, preferred_element_type=F32)


def _dot_nt(a, b):
    return lax.dot_general(a, b, (((1,), (1,)), ((), ())), preferred_element_type=F32)


def _dot_tn(a, b):
    return lax.dot_general(a, b, (((0,), (0,)), ((), ())), preferred_element_type=F32)


def _adaln_kernel(c_ref, w_ref, b_ref, o_ref):
    s = _silu(c_ref[...]).astype(BF16)
    o_ref[0] = _dot(s, w_ref[0].astype(BF16)) + b_ref[0]


def _adaln(cc, ada_w, ada_b):
    depth, d, n3 = ada_w.shape
    tn = 768
    return pl.pallas_call(
        _adaln_kernel,
        grid=(depth, n3 // tn),
        in_specs=[pl.BlockSpec((16, d), lambda i, j: (0, 0)),
                  pl.BlockSpec((1, d, tn), lambda i, j: (i, 0, j)),
                  pl.BlockSpec((1, 1, tn), lambda i, j: (i, 0, j))],
        out_specs=pl.BlockSpec((1, 16, tn), lambda i, j: (i, 0, j)),
        out_shape=jax.ShapeDtypeStruct((depth, 16, n3), F32),
        compiler_params=_cparams(("arbitrary", "arbitrary")),
        name="adaln",
    )(cc, ada_w, ada_b.reshape(depth, 1, n3))


def _mod_row(sub_idx, subs_per_batch):
    b = sub_idx // subs_per_batch
    is_lat = (sub_idx % subs_per_batch != 0).astype(jnp.int32)
    return b * 2 + is_lat


def _inproj_kernel(x_ref, ms_ref, nw_ref, w_ref, ws_ref, z_ref, zs_ref, h_ref, *, n_sub, subs_per_batch):
    i = pl.program_id(0)

    @pl.when(pl.program_id(1) == 0)
    def _():
        for u in range(n_sub):
            m = ms_ref[_mod_row(i * n_sub + u, subs_per_batch)]
            x = x_ref[u * SUB:(u + 1) * SUB, :]
            y = x * lax.rsqrt(jnp.mean(x * x, axis=-1, keepdims=True) + EPS) * nw_ref[...]
            h_ref[u * SUB:(u + 1) * SUB, :] = (y * (1.0 + m[0:1]) + m[1:2]).astype(BF16)
        zs_ref[...] = _dot(h_ref[...], ws_ref[...])

    z_ref[...] = _dot(h_ref[...], w_ref[...]).astype(BF16)


def _inproj(xa, ms, norm_w, w_main, w_small, subs_per_batch):
    bt, d = xa.shape
    tm = 1024 if bt % 1024 == 0 else SUB
    tn = 1280
    return pl.pallas_call(
        functools.partial(_inproj_kernel, n_sub=tm // SUB, subs_per_batch=subs_per_batch),
        grid=(bt // tm, NZ // tn),
        in_specs=[pl.BlockSpec((tm, d), lambda i, j: (i, 0)),
                  pl.BlockSpec(ms.shape, lambda i, j: (0, 0, 0)),
                  pl.BlockSpec((1, d), lambda i, j: (0, 0)),
                  pl.BlockSpec((d, tn), lambda i, j: (0, j)),
                  pl.BlockSpec((d, NZS), lambda i, j: (0, 0))],
        out_specs=[pl.BlockSpec((tm, tn), lambda i, j: (i, j)),
                   pl.BlockSpec((tm, NZS), lambda i, j: (i, 0))],
        out_shape=[jax.ShapeDtypeStruct((bt, NZ), BF16),
                   jax.ShapeDtypeStruct((bt, NZS), F32)],
        scratch_shapes=[pltpu.VMEM((tm, d), BF16)],
        compiler_params=_cparams(("arbitrary", "arbitrary")),
        name="inproj",
    )(xa, ms, norm_w.reshape(1, d), w_main, w_small)


def _outproj_kernel(x_ref, r_ref, g_ref, m_ref, gate_ref, w_ref, fw_ref, o_ref, *, n_sub, subs_per_batch, final):
    i = pl.program_id(0)
    acc = (_dot(r_ref[...], w_ref[0:512, :]) + _dot(g_ref[...], w_ref[512:1024, :])
           + _dot(m_ref[...], w_ref[1024:2048, :]))
    for u in range(n_sub):
        row = (i // (subs_per_batch - 1)) * 2 + 1 if final else _mod_row(i * n_sub + u, subs_per_batch)
        gate = gate_ref[row]
        rows = slice(u * SUB, (u + 1) * SUB)
        xn = x_ref[rows, :] + gate * acc[rows, :]
        if final:
            xn = xn * lax.rsqrt(jnp.mean(xn * xn, axis=-1, keepdims=True) + EPS) * fw_ref[...]
        o_ref[rows, :] = xn


def _outproj(xa, mret, mgdn, mmla, gate_tab, w_out, final_w, subs_per_batch, final):
    bt, d = xa.shape
    if final:
        tm, lat = SUB, subs_per_batch - 1
        n_blocks = (bt // SUB // subs_per_batch) * lat
        src = lambda i: ((i // lat) * subs_per_batch + 1 + i % lat, 0)
        aliases = {}
    else:
        tm = 512 if bt % 512 == 0 else SUB
        n_blocks = bt // tm
        src = lambda i: (i, 0)
        aliases = {0: 0}
    return pl.pallas_call(
        functools.partial(_outproj_kernel, n_sub=tm // SUB, subs_per_batch=subs_per_batch, final=final),
        grid=(n_blocks,),
        in_specs=[pl.BlockSpec((tm, d), src),
                  pl.BlockSpec((tm, 512), src),
                  pl.BlockSpec((tm, 512), src),
                  pl.BlockSpec((tm, 1024), src),
                  pl.BlockSpec(gate_tab.shape, lambda i: (0, 0, 0)),
                  pl.BlockSpec((d, d), lambda i: (0, 0)),
                  pl.BlockSpec((1, d), lambda i: (0, 0))],
        out_specs=pl.BlockSpec((tm, d), lambda i: (i, 0)),
        out_shape=jax.ShapeDtypeStruct((n_blocks * tm, d), F32),
        input_output_aliases=aliases,
        compiler_params=_cparams(("arbitrary",)),
        name="outproj",
    )(xa, mret, mgdn, mmla, gate_tab, w_out, final_w.reshape(1, d))


def _swap_halves(x, half):
    lane = lax.broadcasted_iota(jnp.int32, x.shape, x.ndim - 1)
    first = (lane % (2 * half)) < half
    return jnp.where(first, pltpu.roll(x, 128 - half, x.ndim - 1), pltpu.roll(x, half, x.ndim - 1))


def _rope_tables(n_ctx, seq, dim):
    rows = seq // GRID_W
    row = jnp.repeat(jnp.arange(rows), GRID_W).astype(F32)
    col = jnp.tile(jnp.arange(GRID_W), rows).astype(F32)
    m = dim // 2
    inv = ROPE_BASE ** (-jnp.arange(m // 2, dtype=F32) / (m // 2))
    ar, ac = row[:, None] * inv, col[:, None] * inv
    cos = jnp.concatenate([jnp.cos(ar), jnp.cos(ar), jnp.cos(ac), jnp.cos(ac)], axis=-1)
    sin = jnp.concatenate([-jnp.sin(ar), jnp.sin(ar), -jnp.sin(ac), jnp.sin(ac)], axis=-1)
    cos = jnp.pad(cos, ((n_ctx, 0), (0, 128 - dim)), constant_values=1.0)
    sin = jnp.pad(sin, ((n_ctx, 0), (0, 128 - dim)))
    return cos, sin


def _ret_kernel(dec_ref, q_ref, k_ref, v_ref, g_ref, cos_ref, sin_ref, nw_ref, o_ref,
                acc_ref, qd_ref, inc_ref, st_ref, *, n_chunks):
    h = pl.program_id(1)
    gf = dec_ref[0, h]
    gb = dec_ref[1, h]
    c_len = SUB

    ri = lax.broadcasted_iota(jnp.int32, (c_len, c_len), 0)
    ci = lax.broadcasted_iota(jnp.int32, (c_len, c_len), 1)
    dist = (ri - ci).astype(F32)
    mask = (jnp.where(ri >= ci, jnp.exp(gf * jnp.maximum(dist, 0.0)), 0.0)
            + jnp.where(ri <= ci, jnp.exp(gb * jnp.maximum(-dist, 0.0)), 0.0))
    pos = lax.broadcasted_iota(jnp.int32, (c_len, HD), 0).astype(F32)
    q_dec_f = jnp.exp(gf * (pos + 1.0))
    q_dec_b = jnp.exp(gb * (c_len - pos))
    k_dec_f = jnp.exp(gf * (c_len - 1.0 - pos))
    k_dec_b = jnp.exp(gb * pos)
    s_dec_f = jnp.exp(gf * c_len)
    s_dec_b = jnp.exp(gb * c_len)

    def rows_of(c, n=c_len):
        return pl.ds(c * n if isinstance(c, int) else pl.multiple_of(c * n, n), n)

    grp = math.gcd(n_chunks - 1, 4)

    def for_chunks(fn):
        fn(0)

        def body(g, _):
            for r in range(grp):
                fn(1 + g * grp + r)
            return 0

        lax.fori_loop(0, (n_chunks - 1) // grp, body, 0)

    def local(c):
        rows = rows_of(c)
        cos, sin = cos_ref[rows, :], sin_ref[rows, :]
        q = q_ref[0, rows, :].astype(F32)
        k = k_ref[0, rows, :].astype(F32)
        q = q * cos + _swap_halves(q, 32) * sin
        k = (k * cos + _swap_halves(k, 32) * sin) * (HD ** -0.5)
        v = v_ref[0, rows, :]
        p = (_dot_nt(q.astype(BF16), k.astype(BF16)) * mask).astype(BF16)
        acc_ref[rows, :] = _dot(p, v)
        qd_ref[0, rows, :] = (q * q_dec_f).astype(BF16)
        qd_ref[1, rows, :] = (q * q_dec_b).astype(BF16)
        inc_ref[0, rows_of(c, HD), :] = _dot_tn((k * k_dec_f).astype(BF16), v)
        inc_ref[1, rows_of(c, HD), :] = _dot_tn((k * k_dec_b).astype(BF16), v)

    for_chunks(local)

    def scan(i, carry):
        s_f, s_b = carry
        cb = jnp.where(i == 0, 0, n_chunks - i)
        st_ref[0, rows_of(i, HD), :] = s_f.astype(BF16)
        st_ref[1, rows_of(cb, HD), :] = s_b.astype(BF16)
        return (s_f * s_dec_f + inc_ref[0, rows_of(i, HD), :], s_b * s_dec_b + inc_ref[1, rows_of(cb, HD), :])

    lax.fori_loop(0, n_chunks, scan, (jnp.zeros((HD, HD), F32), jnp.zeros((HD, HD), F32)))

    def finish(c):
        rows = rows_of(c)
        o = (acc_ref[rows, :] + _dot(qd_ref[0, rows, :], st_ref[0, rows_of(c, HD), :])
             + _dot(qd_ref[1, rows, :], st_ref[1, rows_of(c, HD), :]))
        y = o * lax.rsqrt(jnp.mean(o * o, axis=-1, keepdims=True) + EPS) * nw_ref[...]
        o_ref[0, rows, :] = (_silu(g_ref[0, rows, :].astype(F32)) * y).astype(BF16)

    for_chunks(finish)


def _retention(z3, log_decay, norm_w, cos, sin):
    b, t, _ = z3.shape
    cb = COL_RET // HD

    def col(off):
        return pl.BlockSpec((1, t, HD), lambda i, h: (i, 0, cb + off + h))

    return pl.pallas_call(
        functools.partial(_ret_kernel, n_chunks=t // SUB),
        grid=(b, RET_HEADS),
        in_specs=[pl.BlockSpec(memory_space=pltpu.SMEM),
                  col(0), col(4), col(8), col(12),
                  pl.BlockSpec((t, HD), lambda i, h: (0, 0)),
                  pl.BlockSpec((t, HD), lambda i, h: (0, 0)),
                  pl.BlockSpec((1, HD), lambda i, h: (0, 0))],
        out_specs=pl.BlockSpec((1, t, HD), lambda i, h: (i, 0, h)),
        out_shape=jax.ShapeDtypeStruct((b, t, RET_HEADS * HD), BF16),
        scratch_shapes=[pltpu.VMEM((t, HD), F32), pltpu.VMEM((2, t, HD), BF16),
                        pltpu.VMEM((2, (t // SUB) * HD, HD), F32), pltpu.VMEM((2, (t // SUB) * HD, HD), BF16)],
        compiler_params=_cparams(("arbitrary", "arbitrary")),
        name="retention",
    )(log_decay, z3, z3, z3, z3, cos, sin, norm_w.reshape(1, HD))


LANE_GC = 64
LANE_BETA = 72


def _gdn_gate_kernel(zs_ref, alog_ref, dtb_ref, col_ref, row_ref):
    x = zs_ref[...]
    z = x + dtb_ref[...]
    softplus = jnp.maximum(z, 0.0) + jnp.log(1.0 + jnp.exp(-jnp.abs(z)))
    g = -jnp.exp(alog_ref[...]) * softplus
    c = GDN_CHUNK
    pos = lax.broadcasted_iota(jnp.int32, (SUB, NZS), 0) % c
    lane = lax.broadcasted_iota(jnp.int32, (SUB, NZS), 1)
    pre = suf = g
    sh = 1
    while sh < c:
        pre = pre + jnp.where(pos >= sh, pltpu.roll(pre, sh, 0), 0.0)
        suf = suf + jnp.where(pos < c - sh, pltpu.roll(suf, SUB - sh, 0), 0.0)
        sh *= 2
    beta = 1.0 / (1.0 + jnp.exp(-x))
    cols = jnp.where(lane < LANE_GC + GDN_HEADS, pre, jnp.where(lane < LANE_BETA, suf, beta))
    col_ref[...] = cols
    row_ref[0] = cols.T


def _gdn_gates(zs, alog_lane, dtb_lane, b, t):
    spb = t // SUB
    return pl.pallas_call(
        _gdn_gate_kernel,
        grid=(b * spb,),
        in_specs=[pl.BlockSpec((SUB, NZS), lambda i: (i, 0)),
                  pl.BlockSpec((1, NZS), lambda i: (0, 0)),
                  pl.BlockSpec((1, NZS), lambda i: (0, 0))],
        out_specs=[pl.BlockSpec((SUB, NZS), lambda i: (i, 0)),
                   pl.BlockSpec((1, NZS, SUB), lambda i: (i // spb, 0, i % spb))],
        out_shape=[jax.ShapeDtypeStruct((b * t, NZS), F32),
                   jax.ShapeDtypeStruct((b, NZS, t), F32)],
        compiler_params=_cparams(("arbitrary",)),
        name="gdn_gates",
    )(zs, alog_lane, dtb_lane)


GDN_GROUP = 17


def _gdn_kernel(q_ref, k_ref, v_ref, g_ref, cwq_ref, cwk_ref, cwv_ref, col_ref, row_ref, nw_ref, o_ref,
                pad_ref, q32_ref, k32_ref, v32_ref, w_ref, u_ref, qe_ref, sq_ref, sm_ref, att_ref, eg_ref, acc_ref,
                *, n_tok):
    head = pl.program_id(1)
    c_len = GDN_CHUNK
    n_chunks = n_tok // c_len
    n_sub = n_tok // SUB
    grp = GDN_GROUP if n_chunks % GDN_GROUP == 0 else SUB // c_len
    halo = 8

    def pad_row(s):
        return pl.multiple_of(halo + s * SUB + halo * jnp.minimum(s, 1), 8)

    for gap in (0, halo + SUB, 2 * halo + n_tok):
        pad_ref[gap:gap + halo, :] = jnp.zeros((halo, HD), F32)

    for src_ref, cw_ref, dst_ref, norm_scale in ((q_ref, cwq_ref, q32_ref, HD ** -0.5),
                                                 (k_ref, cwk_ref, k32_ref, 1.0),
                                                 (v_ref, cwv_ref, v32_ref, None)):
        def fill(s, _):
            rows = pl.ds(pl.multiple_of(s * SUB, SUB), SUB)
            pad_ref[pl.ds(pad_row(s), SUB), :] = src_ref[0, rows, :].astype(F32)
            return 0

        lax.fori_loop(0, n_sub, fill, 0)
        cw = cw_ref[...]

        def conv(s, _):
            win = pad_ref[pl.ds(pl.multiple_of(pad_row(s) - halo, 8), SUB + 2 * halo), :]
            y = jnp.zeros((SUB, HD), F32)
            for j in range(GDN_CONV):
                off = halo + j - GDN_CONV // 2
                y = y + win[off:off + SUB, :] * cw[j:j + 1, :]
            y = _silu(y)
            if norm_scale is not None:
                y = y * (lax.rsqrt(jnp.sum(y * y, axis=-1, keepdims=True) + EPS) * norm_scale)
            dst_ref[pl.ds(pl.multiple_of(s * SUB, SUB), SUB), :] = y
            return 0

        lax.fori_loop(0, n_sub, conv, 0)

    ri = lax.broadcasted_iota(jnp.int32, (grp, c_len, c_len), 1)
    ci = lax.broadcasted_iota(jnp.int32, (grp, c_len, c_len), 2)
    eye = (ri == ci).astype(F32)
    tri = (ri >= ci, ri <= ci)
    strict = (ri > ci, ri < ci)
    merge_masks = []
    m = 1
    while m < c_len:
        merge_masks.append(((ri // (2 * m)) == (ci // (2 * m))) & ((ri // m) != (ci // m)))
        m *= 2

    def bmm(a, b):
        return lax.dot_general(a, b, (((2,), (1,)), ((0,), (0,))), preferred_element_type=F32)

    def bmm_nt(a, b):
        return lax.dot_general(a, b, (((2,), (2,)), ((0,), (0,))), preferred_element_type=F32)

    def bmm_tn(a, b):
        return lax.dot_general(a, b, (((1,), (1,)), ((0,), (0,))), preferred_element_type=F32)

    def local(gi, _):
        r0 = pl.multiple_of(gi * (grp * c_len), grp * c_len)
        rows = pl.ds(r0, grp * c_len)
        q = q32_ref[rows, :].reshape(grp, c_len, HD)
        k = k32_ref[rows, :].reshape(grp, c_len, HD)
        v = v32_ref[rows, :].reshape(grp, c_len, HD)
        kb16 = k.astype(BF16)
        qk = bmm_nt(q.astype(BF16), kb16)
        cols = col_ref[0, rows, :]
        col_lane = lax.broadcasted_iota(jnp.int32, cols.shape, 1)

        def pick(lane0):
            sel = jnp.where(col_lane == lane0 + head, cols, 0.0)
            return jnp.sum(sel, axis=-1, keepdims=True).reshape(grp, c_len, 1)

        for d in range(2):
            gc = pick(LANE_GC + GDN_HEADS * d)
            beta = pick(LANE_BETA + GDN_HEADS * d)
            gc_row = jnp.stack([row_ref[0, d, 0, pl.ds(gi * grp + g, 1), :] for g in range(grp)], axis=0)
            g_tot = gc[:, c_len - 1:c_len, :] if d == 0 else gc[:, 0:1, :]
            decay = jnp.exp(jnp.where(tri[d], gc - gc_row, -jnp.inf))
            kbeta = k * beta
            a = jnp.where(strict[d], bmm_nt(kbeta.astype(BF16), kb16) * decay, 0.0)
            t = eye - jnp.where(merge_masks[0], a, 0.0)
            for mm in merge_masks[1:]:
                t16 = t.astype(BF16)
                t = t - bmm(bmm(t16, jnp.where(mm, a, 0.0).astype(BF16)).astype(BF16), t16)
            rhs = jnp.concatenate([v * beta, kbeta * jnp.exp(gc)], axis=-1).astype(BF16)
            sol = bmm(t.astype(BF16), rhs)
            u, w16 = sol[:, :, :HD], sol[:, :, HD:].astype(BF16)
            u_ref[d, rows, :] = u.reshape(grp * c_len, HD)
            w_ref[d, rows, :] = w16.reshape(grp * c_len, HD)
            qe_ref[d, rows, :] = (q * jnp.exp(gc)).reshape(grp * c_len, HD).astype(BF16)
            att_ref[d, rows, :] = (qk * decay).reshape(grp * c_len, c_len).astype(BF16)
            kd = (k * jnp.exp(g_tot - gc)).astype(BF16)
            srows = pl.ds(pl.multiple_of(gi * (grp * HD), grp * HD), grp * HD)
            sq_ref[d, srows, :] = bmm_tn(kd, u.astype(BF16)).reshape(grp * HD, HD).astype(BF16)
            sm_ref[d, srows, :] = bmm_tn(kd, w16).reshape(grp * HD, HD).astype(BF16)
            eg_ref[d, pl.ds(pl.multiple_of(gi * grp, grp), grp), :] = jnp.broadcast_to(
                jnp.exp(g_tot).reshape(grp, 1), (grp, HD))
        return 0

    lax.fori_loop(0, n_chunks // grp, local, 0)

    ctx_chunks = SUB // c_len

    def step(d, c, s):
        rows = pl.ds(pl.multiple_of(c * c_len, c_len), c_len)
        srows = pl.ds(pl.multiple_of(c * HD, HD), HD)
        s16 = s.astype(BF16)
        v_new = (u_ref[d, rows, :] - _dot(w_ref[d, rows, :], s16)).astype(BF16)
        acc_ref[d, rows, :] = _dot(qe_ref[d, rows, :], s16) + _dot(att_ref[d, rows, :], v_new)
        return s * eg_ref[d, pl.ds(c, 1), :] + sq_ref[d, srows, :].astype(F32) - _dot(sm_ref[d, srows, :], s16)

    s0 = jnp.zeros((HD, HD), F32)
    scan_unroll = grp

    def scan(i, carry):
        s_f, s_b = carry
        for r in range(scan_unroll):
            j = i * scan_unroll + r
            s_f = step(0, j, s_f)
            s_b = step(1, jnp.where(j < ctx_chunks, ctx_chunks - 1 - j, n_chunks - 1 + ctx_chunks - j), s_b)
        return s_f, s_b

    lax.fori_loop(0, n_chunks // scan_unroll, scan, (s0, s0))

    def finish(s, _):
        rows = pl.ds(pl.multiple_of(s * SUB, SUB), SUB)
        o = acc_ref[0, rows, :] + acc_ref[1, rows, :]
        y = o * lax.rsqrt(jnp.mean(o * o, axis=-1, keepdims=True) + EPS) * nw_ref[...]
        o_ref[0, rows, :] = (_silu(g_ref[0, rows, :].astype(F32)) * y).astype(BF16)
        return 0

    lax.fori_loop(0, n_sub, finish, 0)


def _gdn(z3, conv_w, cols, rows, norm_w):
    b, t, _ = z3.shape
    cb = COL_GDN // HD
    n_chunks = t // GDN_CHUNK

    once = pl.Buffered(1)

    def col(off):
        return pl.BlockSpec((1, t, HD), lambda i, h: (i, 0, cb + off + h), pipeline_mode=once)

    def cw(off):
        return pl.BlockSpec((GDN_CONV, HD), lambda i, h: (0, off + h))

    return pl.pallas_call(
        functools.partial(_gdn_kernel, n_tok=t),
        grid=(b, GDN_HEADS),
        in_specs=[col(0), col(4), col(8), col(12), cw(0), cw(4), cw(8),
                  pl.BlockSpec((1, t, NZS), lambda i, h: (i, 0, 0), pipeline_mode=once),
                  pl.BlockSpec((1, 2, 1, n_chunks, GDN_CHUNK), lambda i, h: (i, 0, h, 0, 0)),
                  pl.BlockSpec((1, HD), lambda i, h: (0, 0))],
        out_specs=pl.BlockSpec((1, t, HD), lambda i, h: (i, 0, h)),
        out_shape=jax.ShapeDtypeStruct((b, t, GDN_HEADS * HD), BF16),
        scratch_shapes=[pltpu.VMEM((t + 24, HD), F32),
                        pltpu.VMEM((t, HD), F32), pltpu.VMEM((t, HD), F32), pltpu.VMEM((t, HD), F32),
                        pltpu.VMEM((2, t, HD), BF16), pltpu.VMEM((2, t, HD), F32),
                        pltpu.VMEM((2, t, HD), BF16),
                        pltpu.VMEM((2, n_chunks * HD, HD), BF16), pltpu.VMEM((2, n_chunks * HD, HD), BF16),
                        pltpu.VMEM((2, t, GDN_CHUNK), BF16),
                        pltpu.VMEM((2, n_chunks, HD), F32),
                        pltpu.VMEM((2, t, HD), F32)],
        compiler_params=_cparams(("arbitrary", "arbitrary")),
        name="gdn",
    )(z3, z3, z3, z3, conv_w, conv_w, conv_w, cols, rows, norm_w.reshape(1, HD))


def _mla_prep_kernel(cq_ref, ckv_ref, zs_ref, qnw_ref, kvnw_ref, wq_ref, wkv_ref, cos_ref, sin_ref,
                     q_ref, k_ref, v_ref):
    def rms(x, w):
        return (x * lax.rsqrt(jnp.mean(x * x, axis=-1, keepdims=True) + EPS) * w).astype(BF16)

    cos, sin = cos_ref[...], sin_ref[...]

    def rope(x):
        return x * cos + _swap_halves(x, 16) * sin

    q = _dot(rms(cq_ref[...].astype(F32), qnw_ref[...]), wq_ref[...])
    kv = _dot(rms(ckv_ref[...].astype(F32), kvnw_ref[...]), wkv_ref[...])
    lane = lax.broadcasted_iota(jnp.int32, (SUB, HD), 1)
    kr = rope(jnp.where(lane < MLA_ROPE, zs_ref[...], 0.0)).astype(BF16)
    for h in range(MLA_HEADS):
        base = h * MLA_QK
        q_ref[0, h, :, 0:HD] = (q[:, base:base + HD] * MLA_Q_SCALE).astype(BF16)
        q_ref[0, h, :, HD:MLA_QK] = (rope(q[:, base + HD:base + MLA_QK]) * MLA_Q_SCALE).astype(BF16)
        k_ref[0, h, :, 0:HD] = kv[:, h * HD:(h + 1) * HD].astype(BF16)
        k_ref[0, h, :, HD:MLA_QK] = kr
        v_ref[0, h] = kv[:, (MLA_HEADS + h) * HD:(MLA_HEADS + h + 1) * HD].astype(BF16)


def _mla_prep(z, zs, qnw, kvnw, wq, wkv, cos, sin, b, t):
    spb = t // SUB

    def tok(width, blk):
        return pl.BlockSpec((SUB, width), lambda i: (i, blk))

    def par(shape):
        return pl.BlockSpec(shape, lambda i: (0, 0))

    def head_out(width):
        return pl.BlockSpec((1, MLA_HEADS, SUB, width), lambda i: (i // spb, 0, i % spb, 0))

    return pl.pallas_call(
        _mla_prep_kernel,
        grid=(b * spb,),
        in_specs=[tok(MLA_Q_RANK, COL_CQ // MLA_Q_RANK), tok(MLA_KV_RANK, COL_CKV // MLA_KV_RANK), tok(NZS, 0),
                  par((1, MLA_Q_RANK)), par((1, MLA_KV_RANK)),
                  par(wq.shape), par(wkv.shape),
                  pl.BlockSpec((SUB, HD), lambda i: (i % spb, 0)),
                  pl.BlockSpec((SUB, HD), lambda i: (i % spb, 0))],
        out_specs=[head_out(MLA_QK), head_out(MLA_QK), head_out(HD)],
        out_shape=[jax.ShapeDtypeStruct((b, MLA_HEADS, t, MLA_QK), BF16),
                   jax.ShapeDtypeStruct((b, MLA_HEADS, t, MLA_QK), BF16),
                   jax.ShapeDtypeStruct((b, MLA_HEADS, t, HD), BF16)],
        compiler_params=_cparams(("arbitrary",)),
        name="mla_prep",
    )(z, z, zs, qnw.reshape(1, -1), kvnw.reshape(1, -1), wq, wkv, cos, sin)


MLA_HEADS_PER_STEP = 8


def _mla_attn_kernel(q_ref, k_ref, v_ref, *gate_and_out_refs):
    g_refs, o_ref = gate_and_out_refs[:-1], gate_and_out_refs[-1]

    def attend(n_keys):
        for hh in range(MLA_HEADS_PER_STEP):
            s = _dot_nt(q_ref[0, hh], k_ref[0, hh, 0:n_keys, :])
            p = jnp.exp2(s - jnp.max(s, axis=-1, keepdims=True))
            l = jnp.sum(p, axis=-1, keepdims=True)
            o = _dot(p.astype(BF16), v_ref[0, hh, 0:n_keys, :]) / l
            gate = _silu(g_refs[hh // 2][0, :, (hh % 2) * HD:(hh % 2 + 1) * HD].astype(F32))
            o_ref[0, :, hh * HD:(hh + 1) * HD] = (gate * o).astype(BF16)

    @pl.when(pl.program_id(2) == 0)
    def _():
        attend(SUB)

    @pl.when(pl.program_id(2) > 0)
    def _():
        attend(k_ref.shape[2])


def _mla_attn(qp, kp, vp, z3):
    b, nh, t, _ = qp.shape
    hs = MLA_HEADS_PER_STEP
    assert hs % 2 == 0 and nh % hs == 0 and COL_MGATE % (2 * HD) == 0
    gb = COL_MGATE // (2 * HD)
    once = pl.Buffered(1)

    def gate_spec(pair):
        return pl.BlockSpec((1, SUB, 2 * HD), lambda i, h, j: (i, j, gb + (hs // 2) * h + pair))

    return pl.pallas_call(
        _mla_attn_kernel,
        grid=(b, nh // hs, t // SUB),
        in_specs=[pl.BlockSpec((1, hs, SUB, MLA_QK), lambda i, h, j: (i, h, j, 0)),
                  pl.BlockSpec((1, hs, t, MLA_QK), lambda i, h, j: (i, h, 0, 0), pipeline_mode=once),
                  pl.BlockSpec((1, hs, t, HD), lambda i, h, j: (i, h, 0, 0), pipeline_mode=once)]
                 + [gate_spec(pair) for pair in range(hs // 2)],
        out_specs=pl.BlockSpec((1, SUB, hs * HD), lambda i, h, j: (i, j, h)),
        out_shape=jax.ShapeDtypeStruct((b, t, nh * HD), BF16),
        compiler_params=_cparams(("arbitrary", "arbitrary", "arbitrary")),
        name="mla_attn",
    )(qp, kp, vp, *([z3] * (hs // 2)))


def _layout_w_in(w_in):
    sizes = (512, 512, 512, 512, 512, 512, 512, 512, 8, 8, MLA_Q_RANK, MLA_KV_RANK, MLA_ROPE, 1024)
    offs = np.concatenate([[0], np.cumsum(sizes)])
    part = lambda n: w_in[:, offs[n]:offs[n + 1]]
    w_main = jnp.concatenate([part(n) for n in (0, 1, 2, 3, 4, 5, 6, 7, 11, 10, 13)], axis=-1)
    w_small = jnp.concatenate([part(12), part(8), part(9)], axis=-1)
    w_small = jnp.pad(w_small, ((0, 0), (0, NZS - w_small.shape[-1])))
    return w_main.astype(BF16), w_small.astype(BF16)


def _layout_w_qb(w_qb):
    w = w_qb.reshape(MLA_Q_RANK, MLA_HEADS, HD + MLA_ROPE)
    w = jnp.pad(w, ((0, 0), (0, 0), (0, MLA_QK - HD - MLA_ROPE)))
    return w.reshape(MLA_Q_RANK, MLA_HEADS * MLA_QK).astype(BF16)


def _layout_w_kvb(w_kvb):
    w = w_kvb.reshape(MLA_KV_RANK, MLA_HEADS, 2, HD)
    return jnp.transpose(w, (0, 2, 1, 3)).reshape(MLA_KV_RANK, 2 * MLA_HEADS * HD).astype(BF16)


def kernel(x, c, ctx, c_ctx, ada_w, ada_b, norm_w, w_in, ret_log_decay, ret_norm_w, gdn_conv_w, gdn_a_log,
           gdn_dt_bias, gdn_norm_w, mla_q_norm_w, mla_w_qb, mla_kv_norm_w, mla_w_kvb, w_out, final_norm_w):
    b, seq, d = x.shape
    n_ctx = ctx.shape[1]
    assert n_ctx == SUB and seq % SUB == 0 and d == D_MODEL
    depth = ada_w.shape[0]
    t = n_ctx + seq
    spb = t // SUB
    n_chunks = t // GDN_CHUNK

    cc = jnp.zeros((16, d), F32).at[:b].set(c).at[b].set(c_ctx)
    mod = _adaln(cc, ada_w, ada_b)
    sel = np.stack([np.full(b, b), np.arange(b)], axis=1).reshape(-1)

    ret_cos, ret_sin = _rope_tables(n_ctx, seq, HD)
    mla_cos, mla_sin = _rope_tables(n_ctx, seq, MLA_ROPE)

    xa = jnp.concatenate([ctx, x], axis=1).reshape(b * t, d)
    for i in range(depth):
        final = i == depth - 1
        shift, scale, gate = mod[i, :, :d], mod[i, :, d:2 * d], mod[i, :, 2 * d:]
        ms = jnp.stack([scale[sel], shift[sel]], axis=1)
        gate_tab = gate[sel][:, None, :]
        w_main, w_small = _layout_w_in(w_in[i])

        z, zs = _inproj(xa, ms, norm_w[i], w_main, w_small, spb)
        z3 = z.reshape(b, t, NZ)

        m_ret = _retention(z3, ret_log_decay[i], ret_norm_w[i], ret_cos, ret_sin)

        on_lanes = lambda p: jnp.zeros((1, NZS), F32).at[0, LANE_GC:LANE_GC + 2 * GDN_HEADS].set(p.reshape(-1))
        cols, rows_t = _gdn_gates(zs, on_lanes(gdn_a_log[i]), on_lanes(gdn_dt_bias[i]), b, t)
        gc_rows = rows_t[:, LANE_GC:LANE_GC + 2 * GDN_HEADS].reshape(b, 2, GDN_HEADS, n_chunks, GDN_CHUNK)
        m_gdn = _gdn(z3, gdn_conv_w[i], cols.reshape(b, t, NZS), gc_rows, gdn_norm_w[i])

        qp, kp, vp = _mla_prep(z, zs, mla_q_norm_w[i], mla_kv_norm_w[i], _layout_w_qb(mla_w_qb[i]),
                               _layout_w_kvb(mla_w_kvb[i]), mla_cos, mla_sin, b, t)
        m_mla = _mla_attn(qp, kp, vp, z3)

        xa = _outproj(xa, m_ret.reshape(b * t, -1), m_gdn.reshape(b * t, -1), m_mla.reshape(b * t, -1),
                      gate_tab, w_out[i].astype(BF16), final_norm_w, spb, final)
    return xa.reshape(b, seq, d)
```

```python
import functools
import math

import jax
import jax.numpy as jnp
import numpy as np
from jax import lax
from jax.experimental import pallas as pl
from jax.experimental.pallas import tpu as pltpu

F32 = jnp.float32
BF16 = jnp.bfloat16

EPS = 1e-6
ROPE_BASE = 10000.0
GRID_W = 64

D_MODEL = 2048
SUB = 256
RET_HEADS = 4
GDN_HEADS = 4
HD = 128
GDN_CONV = 5
GDN_CHUNK = 64
MLA_HEADS = 8
MLA_ROPE = 64
MLA_Q_RANK = 768
MLA_KV_RANK = 512
MLA_QK = 256
MLA_SCALE = (HD + MLA_ROPE) ** -0.5
MLA_Q_SCALE = MLA_SCALE * math.log2(math.e)

COL_RET = 0
COL_GDN = 2048
COL_CKV = 4096
COL_CQ = 4608
COL_MGATE = 5376
NZ = 6400
NZS = 128

VMEM_LIMIT = 56 * 1024 * 1024


def _cparams(sem):
    return pltpu.CompilerParams(dimension_semantics=sem, vmem_limit_bytes=VMEM_LIMIT)


def _silu(x):
    return x * (1.0 / (1.0 + jnp.exp(-x)))


def _dot(a, b):
    return jnp.dot(a, b, preferred_element_type=F32)


def _dot_nt(a, b):
    return lax.dot_general(a, b, (((1,), (1,)), ((), ())), preferred_element_type=F32)


def _dot_tn(a, b):
    return lax.dot_general(a, b, (((0,), (0,)), ((), ())), preferred_element_type=F32)


def _adaln_kernel(c_ref, w_ref, b_ref, o_ref):
    s = _silu(c_ref[...]).astype(BF16)
    o_ref[0] = _dot(s, w_ref[0].astype(BF16)) + b_ref[0]


def _adaln(cc, ada_w, ada_b):
    depth, d, n3 = ada_w.shape
    tn = 768
    return pl.pallas_call(
        _adaln_kernel,
        grid=(depth, n3 // tn),
        in_specs=[pl.BlockSpec((16, d), lambda i, j: (0, 0)),
                  pl.BlockSpec((1, d, tn), lambda i, j: (i, 0, j)),
                  pl.BlockSpec((1, 1, tn), lambda i, j: (i, 0, j))],
        out_specs=pl.BlockSpec((1, 16, tn), lambda i, j: (i, 0, j)),
        out_shape=jax.ShapeDtypeStruct((depth, 16, n3), F32),
        compiler_params=_cparams(("arbitrary", "arbitrary")),
        name="adaln",
    )(cc, ada_w, ada_b.reshape(depth, 1, n3))


def _mod_row(sub_idx, subs_per_batch):
    b = sub_idx // subs_per_batch
    is_lat = (sub_idx % subs_per_batch != 0).astype(jnp.int32)
    return b * 2 + is_lat


def _inproj_kernel(x_ref, ms_ref, nw_ref, w_ref, ws_ref, z_ref, zs_ref, h_ref, *, n_sub, subs_per_batch):
    i = pl.program_id(0)

    @pl.when(pl.program_id(1) == 0)
    def _():
        for u in range(n_sub):
            m = ms_ref[_mod_row(i * n_sub + u, subs_per_batch)]
            x = x_ref[u * SUB:(u + 1) * SUB, :]
            y = x * lax.rsqrt(jnp.mean(x * x, axis=-1, keepdims=True) + EPS) * nw_ref[...]
            h_ref[u * SUB:(u + 1) * SUB, :] = (y * (1.0 + m[0:1]) + m[1:2]).astype(BF16)
        zs_ref[...] = _dot(h_ref[...], ws_ref[...])

    z_ref[...] = _dot(h_ref[...], w_ref[...]).astype(BF16)


def _inproj(xa, ms, norm_w, w_main, w_small, subs_per_batch):
    bt, d = xa.shape
    tm = 1024 if bt % 1024 == 0 else SUB
    tn = 1280
    return pl.pallas_call(
        functools.partial(_inproj_kernel, n_sub=tm // SUB, subs_per_batch=subs_per_batch),
        grid=(bt // tm, NZ // tn),
        in_specs=[pl.BlockSpec((tm, d), lambda i, j: (i, 0)),
                  pl.BlockSpec(ms.shape, lambda i, j: (0, 0, 0)),
                  pl.BlockSpec((1, d), lambda i, j: (0, 0)),
                  pl.BlockSpec((d, tn), lambda i, j: (0, j)),
                  pl.BlockSpec((d, NZS), lambda i, j: (0, 0))],
        out_specs=[pl.BlockSpec((tm, tn), lambda i, j: (i, j)),
                   pl.BlockSpec((tm, NZS), lambda i, j: (i, 0))],
        out_shape=[jax.ShapeDtypeStruct((bt, NZ), BF16),
                   jax.ShapeDtypeStruct((bt, NZS), F32)],
        scratch_shapes=[pltpu.VMEM((tm, d), BF16)],
        compiler_params=_cparams(("arbitrary", "arbitrary")),
        name="inproj",
    )(xa, ms, norm_w.reshape(1, d), w_main, w_small)


def _outproj_kernel(x_ref, r_ref, g_ref, m_ref, gate_ref, w_ref, fw_ref, o_ref, *, n_sub, subs_per_batch, final):
    i = pl.program_id(0)
    acc = (_dot(r_ref[...], w_ref[0:512, :]) + _dot(g_ref[...], w_ref[512:1024, :])
           + _dot(m_ref[...], w_ref[1024:2048, :]))
    for u in range(n_sub):
        row = (i // (subs_per_batch - 1)) * 2 + 1 if final else _mod_row(i * n_sub + u, subs_per_batch)
        gate = gate_ref[row]
        rows = slice(u * SUB, (u + 1) * SUB)
        xn = x_ref[rows, :] + gate * acc[rows, :]
        if final:
            xn = xn * lax.rsqrt(jnp.mean(xn * xn, axis=-1, keepdims=True) + EPS) * fw_ref[...]
        o_ref[rows, :] = xn


def _outproj(xa, mret, mgdn, mmla, gate_tab, w_out, final_w, subs_per_batch, final):
    bt, d = xa.shape
    if final:
        tm, lat = SUB, subs_per_batch - 1
        n_blocks = (bt // SUB // subs_per_batch) * lat
        src = lambda i: ((i // lat) * subs_per_batch + 1 + i % lat, 0)
        aliases = {}
    else:
        tm = 512 if bt % 512 == 0 else SUB
        n_blocks = bt // tm
        src = lambda i: (i, 0)
        aliases = {0: 0}
    return pl.pallas_call(
        functools.partial(_outproj_kernel, n_sub=tm // SUB, subs_per_batch=subs_per_batch, final=final),
        grid=(n_blocks,),
        in_specs=[pl.BlockSpec((tm, d), src),
                  pl.BlockSpec((tm, 512), src),
                  pl.BlockSpec((tm, 512), src),
                  pl.BlockSpec((tm, 1024), src),
                  pl.BlockSpec(gate_tab.shape, lambda i: (0, 0, 0)),
                  pl.BlockSpec((d, d), lambda i: (0, 0)),
                  pl.BlockSpec((1, d), lambda i: (0, 0))],
        out_specs=pl.BlockSpec((tm, d), lambda i: (i, 0)),
        out_shape=jax.ShapeDtypeStruct((n_blocks * tm, d), F32),
        input_output_aliases=aliases,
        compiler_params=_cparams(("arbitrary",)),
        name="outproj",
    )(xa, mret, mgdn, mmla, gate_tab, w_out, final_w.reshape(1, d))


def _swap_halves(x, half):
    lane = lax.broadcasted_iota(jnp.int32, x.shape, x.ndim - 1)
    first = (lane % (2 * half)) < half
    return jnp.where(first, pltpu.roll(x, 128 - half, x.ndim - 1), pltpu.roll(x, half, x.ndim - 1))


def _rope_tables(n_ctx, seq, dim):
    rows = seq // GRID_W
    row = jnp.repeat(jnp.arange(rows), GRID_W).astype(F32)
    col = jnp.tile(jnp.arange(GRID_W), rows).astype(F32)
    m = dim // 2
    inv = ROPE_BASE ** (-jnp.arange(m // 2, dtype=F32) / (m // 2))
    ar, ac = row[:, None] * inv, col[:, None] * inv
    cos = jnp.concatenate([jnp.cos(ar), jnp.cos(ar), jnp.cos(ac), jnp.cos(ac)], axis=-1)
    sin = jnp.concatenate([-jnp.sin(ar), jnp.sin(ar), -jnp.sin(ac), jnp.sin(ac)], axis=-1)
    cos = jnp.pad(cos, ((n_ctx, 0), (0, 128 - dim)), constant_values=1.0)
    sin = jnp.pad(sin, ((n_ctx, 0), (0, 128 - dim)))
    return cos, sin


def _ret_kernel(dec_ref, q_ref, k_ref, v_ref, g_ref, cos_ref, sin_ref, nw_ref, o_ref,
                acc_ref, qd_ref, inc_ref, st_ref, *, n_chunks):
    h = pl.program_id(1)
    gf = dec_ref[0, h]
    gb = dec_ref[1, h]
    c_len = SUB

    ri = lax.broadcasted_iota(jnp.int32, (c_len, c_len), 0)
    ci = lax.broadcasted_iota(jnp.int32, (c_len, c_len), 1)
    dist = (ri - ci).astype(F32)
    mask = (jnp.where(ri >= ci, jnp.exp(gf * jnp.maximum(dist, 0.0)), 0.0)
            + jnp.where(ri <= ci, jnp.exp(gb * jnp.maximum(-dist, 0.0)), 0.0))
    pos = lax.broadcasted_iota(jnp.int32, (c_len, HD), 0).astype(F32)
    q_dec_f = jnp.exp(gf * (pos + 1.0))
    q_dec_b = jnp.exp(gb * (c_len - pos))
    k_dec_f = jnp.exp(gf * (c_len - 1.0 - pos))
    k_dec_b = jnp.exp(gb * pos)
    s_dec_f = jnp.exp(gf * c_len)
    s_dec_b = jnp.exp(gb * c_len)

    def rows_of(c, n=c_len):
        return pl.ds(c * n if isinstance(c, int) else pl.multiple_of(c * n, n), n)

    grp = math.gcd(n_chunks - 1, 8)

    def for_chunks(fn):
        fn(0)

        def body(g, _):
            for r in range(grp):
                fn(1 + g * grp + r)
            return 0

        lax.fori_loop(0, (n_chunks - 1) // grp, body, 0)

    def local(c):
        rows = rows_of(c)
        cos, sin = cos_ref[rows, :], sin_ref[rows, :]
        q = q_ref[0, rows, :].astype(F32)
        k = k_ref[0, rows, :].astype(F32)
        q = q * cos + _swap_halves(q, 32) * sin
        k = (k * cos + _swap_halves(k, 32) * sin) * (HD ** -0.5)
        v = v_ref[0, rows, :]
        p = (_dot_nt(q.astype(BF16), k.astype(BF16)) * mask).astype(BF16)
        acc_ref[rows, :] = _dot(p, v)
        qd_ref[0, rows, :] = (q * q_dec_f).astype(BF16)
        qd_ref[1, rows, :] = (q * q_dec_b).astype(BF16)
        inc_ref[0, rows_of(c, HD), :] = _dot_tn((k * k_dec_f).astype(BF16), v)
        inc_ref[1, rows_of(c, HD), :] = _dot_tn((k * k_dec_b).astype(BF16), v)

    for_chunks(local)

    def scan(i, carry):
        s_f, s_b = carry
        cb = jnp.where(i == 0, 0, n_chunks - i)
        st_ref[0, rows_of(i, HD), :] = s_f.astype(BF16)
        st_ref[1, rows_of(cb, HD), :] = s_b.astype(BF16)
        return (s_f * s_dec_f + inc_ref[0, rows_of(i, HD), :], s_b * s_dec_b + inc_ref[1, rows_of(cb, HD), :])

    lax.fori_loop(0, n_chunks, scan, (jnp.zeros((HD, HD), F32), jnp.zeros((HD, HD), F32)))

    def finish(c):
        rows = rows_of(c)
        o = (acc_ref[rows, :] + _dot(qd_ref[0, rows, :], st_ref[0, rows_of(c, HD), :])
             + _dot(qd_ref[1, rows, :], st_ref[1, rows_of(c, HD), :]))
        y = o * lax.rsqrt(jnp.mean(o * o, axis=-1, keepdims=True) + EPS) * nw_ref[...]
        o_ref[0, rows, :] = (_silu(g_ref[0, rows, :].astype(F32)) * y).astype(BF16)

    for_chunks(finish)


def _retention(z3, log_decay, norm_w, cos, sin):
    b, t, _ = z3.shape
    cb = COL_RET // HD

    def col(off):
        return pl.BlockSpec((1, t, HD), lambda i, h: (i, 0, cb + off + h))

    return pl.pallas_call(
        functools.partial(_ret_kernel, n_chunks=t // SUB),
        grid=(b, RET_HEADS),
        in_specs=[pl.BlockSpec(memory_space=pltpu.SMEM),
                  col(0), col(4), col(8), col(12),
                  pl.BlockSpec((t, HD), lambda i, h: (0, 0)),
                  pl.BlockSpec((t, HD), lambda i, h: (0, 0)),
                  pl.BlockSpec((1, HD), lambda i, h: (0, 0))],
        out_specs=pl.BlockSpec((1, t, HD), lambda i, h: (i, 0, h)),
        out_shape=jax.ShapeDtypeStruct((b, t, RET_HEADS * HD), BF16),
        scratch_shapes=[pltpu.VMEM((t, HD), F32), pltpu.VMEM((2, t, HD), BF16),
                        pltpu.VMEM((2, (t // SUB) * HD, HD), F32), pltpu.VMEM((2, (t // SUB) * HD, HD), BF16)],
        compiler_params=_cparams(("arbitrary", "arbitrary")),
        name="retention",
    )(log_decay, z3, z3, z3, z3, cos, sin, norm_w.reshape(1, HD))


LANE_GC = 64
LANE_BETA = 72


def _gdn_gate_kernel(zs_ref, alog_ref, dtb_ref, col_ref, row_ref):
    x = zs_ref[...]
    z = x + dtb_ref[...]
    softplus = jnp.maximum(z, 0.0) + jnp.log(1.0 + jnp.exp(-jnp.abs(z)))
    g = -jnp.exp(alog_ref[...]) * softplus
    c = GDN_CHUNK
    pos = lax.broadcasted_iota(jnp.int32, (SUB, NZS), 0) % c
    lane = lax.broadcasted_iota(jnp.int32, (SUB, NZS), 1)
    pre = suf = g
    sh = 1
    while sh < c:
        pre = pre + jnp.where(pos >= sh, pltpu.roll(pre, sh, 0), 0.0)
        suf = suf + jnp.where(pos < c - sh, pltpu.roll(suf, SUB - sh, 0), 0.0)
        sh *= 2
    beta = 1.0 / (1.0 + jnp.exp(-x))
    cols = jnp.where(lane < LANE_GC + GDN_HEADS, pre, jnp.where(lane < LANE_BETA, suf, beta))
    col_ref[...] = cols
    row_ref[0] = cols.T


def _gdn_gates(zs, alog_lane, dtb_lane, b, t):
    spb = t // SUB
    return pl.pallas_call(
        _gdn_gate_kernel,
        grid=(b * spb,),
        in_specs=[pl.BlockSpec((SUB, NZS), lambda i: (i, 0)),
                  pl.BlockSpec((1, NZS), lambda i: (0, 0)),
                  pl.BlockSpec((1, NZS), lambda i: (0, 0))],
        out_specs=[pl.BlockSpec((SUB, NZS), lambda i: (i, 0)),
                   pl.BlockSpec((1, NZS, SUB), lambda i: (i // spb, 0, i % spb))],
        out_shape=[jax.ShapeDtypeStruct((b * t, NZS), F32),
                   jax.ShapeDtypeStruct((b, NZS, t), F32)],
        compiler_params=_cparams(("arbitrary",)),
        name="gdn_gates",
    )(zs, alog_lane, dtb_lane)


GDN_GROUP = 34


def _gdn_kernel(q_ref, k_ref, v_ref, g_ref, cwq_ref, cwk_ref, cwv_ref, col_ref, row_ref, nw_ref, o_ref,
                pad_ref, q32_ref, k32_ref, v32_ref, w_ref, u_ref, qe_ref, sq_ref, sm_ref, att_ref, eg_ref, acc_ref,
                *, n_tok):
    head = pl.program_id(1)
    c_len = GDN_CHUNK
    n_chunks = n_tok // c_len
    n_sub = n_tok // SUB
    grp = GDN_GROUP if n_chunks % GDN_GROUP == 0 else SUB // c_len
    halo = 8

    def pad_row(s):
        return pl.multiple_of(halo + s * SUB + halo * jnp.minimum(s, 1), 8)

    for gap in (0, halo + SUB, 2 * halo + n_tok):
        pad_ref[gap:gap + halo, :] = jnp.zeros((halo, HD), F32)

    for src_ref, cw_ref, dst_ref, norm_scale in ((q_ref, cwq_ref, q32_ref, HD ** -0.5),
                                                 (k_ref, cwk_ref, k32_ref, 1.0),
                                                 (v_ref, cwv_ref, v32_ref, None)):
        def fill(s, _):
            rows = pl.ds(pl.multiple_of(s * SUB, SUB), SUB)
            pad_ref[pl.ds(pad_row(s), SUB), :] = src_ref[0, rows, :].astype(F32)
            return 0

        lax.fori_loop(0, n_sub, fill, 0)
        cw = cw_ref[...]

        def conv(s, _):
            win = pad_ref[pl.ds(pl.multiple_of(pad_row(s) - halo, 8), SUB + 2 * halo), :]
            y = jnp.zeros((SUB, HD), F32)
            for j in range(GDN_CONV):
                off = halo + j - GDN_CONV // 2
                y = y + win[off:off + SUB, :] * cw[j:j + 1, :]
            y = _silu(y)
            if norm_scale is not None:
                y = y * (lax.rsqrt(jnp.sum(y * y, axis=-1, keepdims=True) + EPS) * norm_scale)
            dst_ref[pl.ds(pl.multiple_of(s * SUB, SUB), SUB), :] = y
            return 0

        lax.fori_loop(0, n_sub, conv, 0)

    ri = lax.broadcasted_iota(jnp.int32, (grp, c_len, c_len), 1)
    ci = lax.broadcasted_iota(jnp.int32, (grp, c_len, c_len), 2)
    eye = (ri == ci).astype(F32)
    tri = (ri >= ci, ri <= ci)
    strict = (ri > ci, ri < ci)
    merge_masks = []
    m = 1
    while m < c_len:
        merge_masks.append(((ri // (2 * m)) == (ci // (2 * m))) & ((ri // m) != (ci // m)))
        m *= 2

    def bmm(a, b):
        return lax.dot_general(a, b, (((2,), (1,)), ((0,), (0,))), preferred_element_type=F32)

    def bmm_nt(a, b):
        return lax.dot_general(a, b, (((2,), (2,)), ((0,), (0,))), preferred_element_type=F32)

    def bmm_tn(a, b):
        return lax.dot_general(a, b, (((1,), (1,)), ((0,), (0,))), preferred_element_type=F32)

    def local(gi, _):
        r0 = pl.multiple_of(gi * (grp * c_len), grp * c_len)
        rows = pl.ds(r0, grp * c_len)
        q = q32_ref[rows, :].reshape(grp, c_len, HD)
        k = k32_ref[rows, :].reshape(grp, c_len, HD)
        v = v32_ref[rows, :].reshape(grp, c_len, HD)
        kb16 = k.astype(BF16)
        qk = bmm_nt(q.astype(BF16), kb16)
        cols = col_ref[0, rows, :]
        col_lane = lax.broadcasted_iota(jnp.int32, cols.shape, 1)

        def pick(lane0):
            sel = jnp.where(col_lane == lane0 + head, cols, 0.0)
            return jnp.sum(sel, axis=-1, keepdims=True).reshape(grp, c_len, 1)

        for d in range(2):
            gc = pick(LANE_GC + GDN_HEADS * d)
            beta = pick(LANE_BETA + GDN_HEADS * d)
            gc_row = jnp.stack([row_ref[0, d, 0, pl.ds(gi * grp + g, 1), :] for g in range(grp)], axis=0)
            g_tot = gc[:, c_len - 1:c_len, :] if d == 0 else gc[:, 0:1, :]
            decay = jnp.exp(jnp.where(tri[d], gc - gc_row, -jnp.inf))
            kbeta = k * beta
            a = jnp.where(strict[d], bmm_nt(kbeta.astype(BF16), kb16) * decay, 0.0)
            t = eye - jnp.where(merge_masks[0], a, 0.0)
            for mm in merge_masks[1:]:
                t16 = t.astype(BF16)
                t = t - bmm(bmm(t16, jnp.where(mm, a, 0.0).astype(BF16)).astype(BF16), t16)
            rhs = jnp.concatenate([v * beta, kbeta * jnp.exp(gc)], axis=-1).astype(BF16)
            sol = bmm(t.astype(BF16), rhs)
            u, w16 = sol[:, :, :HD], sol[:, :, HD:].astype(BF16)
            u_ref[d, rows, :] = u.reshape(grp * c_len, HD)
            w_ref[d, rows, :] = w16.reshape(grp * c_len, HD)
            qe_ref[d, rows, :] = (q * jnp.exp(gc)).reshape(grp * c_len, HD).astype(BF16)
            att_ref[d, rows, :] = (qk * decay).reshape(grp * c_len, c_len).astype(BF16)
            kd = (k * jnp.exp(g_tot - gc)).astype(BF16)
            srows = pl.ds(pl.multiple_of(gi * (grp * HD), grp * HD), grp * HD)
            sq_ref[d, srows, :] = bmm_tn(kd, u.astype(BF16)).reshape(grp * HD, HD).astype(BF16)
            sm_ref[d, srows, :] = bmm_tn(kd, w16).reshape(grp * HD, HD).astype(BF16)
            eg_ref[d, pl.ds(pl.multiple_of(gi * grp, grp), grp), :] = jnp.broadcast_to(
                jnp.exp(g_tot).reshape(grp, 1), (grp, HD))
        return 0

    lax.fori_loop(0, n_chunks // grp, local, 0)

    ctx_chunks = SUB // c_len

    def step(d, c, s):
        rows = pl.ds(pl.multiple_of(c * c_len, c_len), c_len)
        srows = pl.ds(pl.multiple_of(c * HD, HD), HD)
        s16 = s.astype(BF16)
        v_new = (u_ref[d, rows, :] - _dot(w_ref[d, rows, :], s16)).astype(BF16)
        acc_ref[d, rows, :] = _dot(qe_ref[d, rows, :], s16) + _dot(att_ref[d, rows, :], v_new)
        return s * eg_ref[d, pl.ds(c, 1), :] + sq_ref[d, srows, :].astype(F32) - _dot(sm_ref[d, srows, :], s16)

    s0 = jnp.zeros((HD, HD), F32)
    scan_unroll = grp

    def scan(i, carry):
        s_f, s_b = carry
        for r in range(scan_unroll):
            j = i * scan_unroll + r
            s_f = step(0, j, s_f)
            s_b = step(1, jnp.where(j < ctx_chunks, ctx_chunks - 1 - j, n_chunks - 1 + ctx_chunks - j), s_b)
        return s_f, s_b

    lax.fori_loop(0, n_chunks // scan_unroll, scan, (s0, s0))

    def finish(s, _):
        rows = pl.ds(pl.multiple_of(s * SUB, SUB), SUB)
        o = acc_ref[0, rows, :] + acc_ref[1, rows, :]
        y = o * lax.rsqrt(jnp.mean(o * o, axis=-1, keepdims=True) + EPS) * nw_ref[...]
        o_ref[0, rows, :] = (_silu(g_ref[0, rows, :].astype(F32)) * y).astype(BF16)
        return 0

    lax.fori_loop(0, n_sub, finish, 0)


def _gdn(z3, conv_w, cols, rows, norm_w):
    b, t, _ = z3.shape
    cb = COL_GDN // HD
    n_chunks = t // GDN_CHUNK

    once = pl.Buffered(1)

    def col(off):
        return pl.BlockSpec((1, t, HD), lambda i, h: (i, 0, cb + off + h), pipeline_mode=once)

    def cw(off):
        return pl.BlockSpec((GDN_CONV, HD), lambda i, h: (0, off + h))

    return pl.pallas_call(
        functools.partial(_gdn_kernel, n_tok=t),
        grid=(b, GDN_HEADS),
        in_specs=[col(0), col(4), col(8), col(12), cw(0), cw(4), cw(8),
                  pl.BlockSpec((1, t, NZS), lambda i, h: (i, 0, 0), pipeline_mode=once),
                  pl.BlockSpec((1, 2, 1, n_chunks, GDN_CHUNK), lambda i, h: (i, 0, h, 0, 0)),
                  pl.BlockSpec((1, HD), lambda i, h: (0, 0))],
        out_specs=pl.BlockSpec((1, t, HD), lambda i, h: (i, 0, h)),
        out_shape=jax.ShapeDtypeStruct((b, t, GDN_HEADS * HD), BF16),
        scratch_shapes=[pltpu.VMEM((t + 24, HD), F32),
                        pltpu.VMEM((t, HD), F32), pltpu.VMEM((t, HD), F32), pltpu.VMEM((t, HD), F32),
                        pltpu.VMEM((2, t, HD), BF16), pltpu.VMEM((2, t, HD), F32),
                        pltpu.VMEM((2, t, HD), BF16),
                        pltpu.VMEM((2, n_chunks * HD, HD), BF16), pltpu.VMEM((2, n_chunks * HD, HD), BF16),
                        pltpu.VMEM((2, t, GDN_CHUNK), BF16),
                        pltpu.VMEM((2, n_chunks, HD), F32),
                        pltpu.VMEM((2, t, HD), F32)],
        compiler_params=_cparams(("arbitrary", "arbitrary")),
        name="gdn",
    )(z3, z3, z3, z3, conv_w, conv_w, conv_w, cols, rows, norm_w.reshape(1, HD))


def _mla_prep_kernel(cq_ref, ckv_ref, zs_ref, qnw_ref, kvnw_ref, wq_ref, wkv_ref, cos_ref, sin_ref,
                     q_ref, k_ref, v_ref):
    def rms(x, w):
        return (x * lax.rsqrt(jnp.mean(x * x, axis=-1, keepdims=True) + EPS) * w).astype(BF16)

    cos, sin = cos_ref[...], sin_ref[...]

    def rope(x):
        return x * cos + _swap_halves(x, 16) * sin

    q = _dot(rms(cq_ref[...].astype(F32), qnw_ref[...]), wq_ref[...])
    kv = _dot(rms(ckv_ref[...].astype(F32), kvnw_ref[...]), wkv_ref[...])
    lane = lax.broadcasted_iota(jnp.int32, (SUB, HD), 1)
    kr = rope(jnp.where(lane < MLA_ROPE, zs_ref[...], 0.0)).astype(BF16)
    for h in range(MLA_HEADS):
        base = h * MLA_QK
        q_ref[0, h, :, 0:HD] = (q[:, base:base + HD] * MLA_Q_SCALE).astype(BF16)
        q_ref[0, h, :, HD:MLA_QK] = (rope(q[:, base + HD:base + MLA_QK]) * MLA_Q_SCALE).astype(BF16)
        k_ref[0, h, :, 0:HD] = kv[:, h * HD:(h + 1) * HD].astype(BF16)
        k_ref[0, h, :, HD:MLA_QK] = kr
        v_ref[0, h] = kv[:, (MLA_HEADS + h) * HD:(MLA_HEADS + h + 1) * HD].astype(BF16)


def _mla_prep(z, zs, qnw, kvnw, wq, wkv, cos, sin, b, t):
    spb = t // SUB

    def tok(width, blk):
        return pl.BlockSpec((SUB, width), lambda i: (i, blk))

    def par(shape):
        return pl.BlockSpec(shape, lambda i: (0, 0))

    def head_out(width):
        return pl.BlockSpec((1, MLA_HEADS, SUB, width), lambda i: (i // spb, 0, i % spb, 0))

    return pl.pallas_call(
        _mla_prep_kernel,
        grid=(b * spb,),
        in_specs=[tok(MLA_Q_RANK, COL_CQ // MLA_Q_RANK), tok(MLA_KV_RANK, COL_CKV // MLA_KV_RANK), tok(NZS, 0),
                  par((1, MLA_Q_RANK)), par((1, MLA_KV_RANK)),
                  par(wq.shape), par(wkv.shape),
                  pl.BlockSpec((SUB, HD), lambda i: (i % spb, 0)),
                  pl.BlockSpec((SUB, HD), lambda i: (i % spb, 0))],
        out_specs=[head_out(MLA_QK), head_out(MLA_QK), head_out(HD)],
        out_shape=[jax.ShapeDtypeStruct((b, MLA_HEADS, t, MLA_QK), BF16),
                   jax.ShapeDtypeStruct((b, MLA_HEADS, t, MLA_QK), BF16),
                   jax.ShapeDtypeStruct((b, MLA_HEADS, t, HD), BF16)],
        compiler_params=_cparams(("arbitrary",)),
        name="mla_prep",
    )(z, z, zs, qnw.reshape(1, -1), kvnw.reshape(1, -1), wq, wkv, cos, sin)


MLA_HEADS_PER_STEP = 8


def _mla_attn_kernel(q_ref, k_ref, v_ref, *gate_and_out_refs):
    g_refs, o_ref = gate_and_out_refs[:-1], gate_and_out_refs[-1]

    def attend(n_keys):
        for hh in range(MLA_HEADS_PER_STEP):
            s = _dot_nt(q_ref[0, hh], k_ref[0, hh, 0:n_keys, :])
            p = jnp.exp2(s - jnp.max(s, axis=-1, keepdims=True))
            l = jnp.sum(p, axis=-1, keepdims=True)
            o = _dot(p.astype(BF16), v_ref[0, hh, 0:n_keys, :]) / l
            gate = _silu(g_refs[hh // 2][0, :, (hh % 2) * HD:(hh % 2 + 1) * HD].astype(F32))
            o_ref[0, :, hh * HD:(hh + 1) * HD] = (gate * o).astype(BF16)

    @pl.when(pl.program_id(2) == 0)
    def _():
        attend(SUB)

    @pl.when(pl.program_id(2) > 0)
    def _():
        attend(k_ref.shape[2])


def _mla_attn(qp, kp, vp, z3):
    b, nh, t, _ = qp.shape
    hs = MLA_HEADS_PER_STEP
    assert hs % 2 == 0 and nh % hs == 0 and COL_MGATE % (2 * HD) == 0
    gb = COL_MGATE // (2 * HD)
    once = pl.Buffered(1)

    def gate_spec(pair):
        return pl.BlockSpec((1, SUB, 2 * HD), lambda i, h, j: (i, j, gb + (hs // 2) * h + pair))

    return pl.pallas_call(
        _mla_attn_kernel,
        grid=(b, nh // hs, t // SUB),
        in_specs=[pl.BlockSpec((1, hs, SUB, MLA_QK), lambda i, h, j: (i, h, j, 0)),
                  pl.BlockSpec((1, hs, t, MLA_QK), lambda i, h, j: (i, h, 0, 0), pipeline_mode=once),
                  pl.BlockSpec((1, hs, t, HD), lambda i, h, j: (i, h, 0, 0), pipeline_mode=once)]
                 + [gate_spec(pair) for pair in range(hs // 2)],
        out_specs=pl.BlockSpec((1, SUB, hs * HD), lambda i, h, j: (i, j, h)),
        out_shape=jax.ShapeDtypeStruct((b, t, nh * HD), BF16),
        compiler_params=_cparams(("arbitrary", "arbitrary", "arbitrary")),
        name="mla_attn",
    )(qp, kp, vp, *([z3] * (hs // 2)))


def _layout_w_in(w_in):
    sizes = (512, 512, 512, 512, 512, 512, 512, 512, 8, 8, MLA_Q_RANK, MLA_KV_RANK, MLA_ROPE, 1024)
    offs = np.concatenate([[0], np.cumsum(sizes)])
    part = lambda n: w_in[:, offs[n]:offs[n + 1]]
    w_main = jnp.concatenate([part(n) for n in (0, 1, 2, 3, 4, 5, 6, 7, 11, 10, 13)], axis=-1)
    w_small = jnp.concatenate([part(12), part(8), part(9)], axis=-1)
    w_small = jnp.pad(w_small, ((0, 0), (0, NZS - w_small.shape[-1])))
    return w_main.astype(BF16), w_small.astype(BF16)


def _layout_w_qb(w_qb):
    w = w_qb.reshape(MLA_Q_RANK, MLA_HEADS, HD + MLA_ROPE)
    w = jnp.pad(w, ((0, 0), (0, 0), (0, MLA_QK - HD - MLA_ROPE)))
    return w.reshape(MLA_Q_RANK, MLA_HEADS * MLA_QK).astype(BF16)


def _layout_w_kvb(w_kvb):
    w = w_kvb.reshape(MLA_KV_RANK, MLA_HEADS, 2, HD)
    return jnp.transpose(w, (0, 2, 1, 3)).reshape(MLA_KV_RANK, 2 * MLA_HEADS * HD).astype(BF16)


def kernel(x, c, ctx, c_ctx, ada_w, ada_b, norm_w, w_in, ret_log_decay, ret_norm_w, gdn_conv_w, gdn_a_log,
           gdn_dt_bias, gdn_norm_w, mla_q_norm_w, mla_w_qb, mla_kv_norm_w, mla_w_kvb, w_out, final_norm_w):
    b, seq, d = x.shape
    n_ctx = ctx.shape[1]
    assert n_ctx == SUB and seq % SUB == 0 and d == D_MODEL
    depth = ada_w.shape[0]
    t = n_ctx + seq
    spb = t // SUB
    n_chunks = t // GDN_CHUNK

    cc = jnp.zeros((16, d), F32).at[:b].set(c).at[b].set(c_ctx)
    mod = _adaln(cc, ada_w, ada_b)
    sel = np.stack([np.full(b, b), np.arange(b)], axis=1).reshape(-1)

    ret_cos, ret_sin = _rope_tables(n_ctx, seq, HD)
    mla_cos, mla_sin = _rope_tables(n_ctx, seq, MLA_ROPE)

    xa = jnp.concatenate([ctx, x], axis=1).reshape(b * t, d)
    for i in range(depth):
        final = i == depth - 1
        shift, scale, gate = mod[i, :, :d], mod[i, :, d:2 * d], mod[i, :, 2 * d:]
        ms = jnp.stack([scale[sel], shift[sel]], axis=1)
        gate_tab = gate[sel][:, None, :]
        w_main, w_small = _layout_w_in(w_in[i])

        z, zs = _inproj(xa, ms, norm_w[i], w_main, w_small, spb)
        z3 = z.reshape(b, t, NZ)

        m_ret = _retention(z3, ret_log_decay[i], ret_norm_w[i], ret_cos, ret_sin)

        on_lanes = lambda p: jnp.zeros((1, NZS), F32).at[0, LANE_GC:LANE_GC + 2 * GDN_HEADS].set(p.reshape(-1))
        cols, rows_t = _gdn_gates(zs, on_lanes(gdn_a_log[i]), on_lanes(gdn_dt_bias[i]), b, t)
        gc_rows = rows_t[:, LANE_GC:LANE_GC + 2 * GDN_HEADS].reshape(b, 2, GDN_HEADS, n_chunks, GDN_CHUNK)
        m_gdn = _gdn(z3, gdn_conv_w[i], cols.reshape(b, t, NZS), gc_rows, gdn_norm_w[i])

        qp, kp, vp = _mla_prep(z, zs, mla_q_norm_w[i], mla_kv_norm_w[i], _layout_w_qb(mla_w_qb[i]),
                               _layout_w_kvb(mla_w_kvb[i]), mla_cos, mla_sin, b, t)
        m_mla = _mla_attn(qp, kp, vp, z3)

        xa = _outproj(xa, m_ret.reshape(b * t, -1), m_gdn.reshape(b * t, -1), m_mla.reshape(b * t, -1),
                      gate_tab, w_out[i].astype(BF16), final_norm_w, spb, final)
    return xa.reshape(b, seq, d)
```

```python
import functools
import math

import jax
import jax.numpy as jnp
import numpy as np
from jax import lax
from jax.experimental import pallas as pl
from jax.experimental.pallas import tpu as pltpu

F32 = jnp.float32
BF16 = jnp.bfloat16

EPS = 1e-6
ROPE_BASE = 10000.0
GRID_W = 64

D_MODEL = 2048
SUB = 256
RET_HEADS = 4
GDN_HEADS = 4
HD = 128
GDN_CONV = 5
GDN_CHUNK = 64
MLA_HEADS = 8
MLA_ROPE = 64
MLA_Q_RANK = 768
MLA_KV_RANK = 512
MLA_QK = 256
MLA_SCALE = (HD + MLA_ROPE) ** -0.5
MLA_Q_SCALE = MLA_SCALE * math.log2(math.e)

COL_RET = 0
COL_GDN = 2048
COL_CKV = 4096
COL_CQ = 4608
COL_MGATE = 5376
NZ = 6400
NZS = 128

VMEM_LIMIT = 56 * 1024 * 1024


def _cparams(sem):
    return pltpu.CompilerParams(dimension_semantics=sem, vmem_limit_bytes=VMEM_LIMIT)


def _silu(x):
    return x * (1.0 / (1.0 + jnp.exp(-x)))


def _dot(a, b):
    return jnp.dot(a, b, preferred_element_type=F32)


def _dot_nt(a, b):
    return lax.dot_general(a, b, (((1,), (1,)), ((), ())), preferred_element_type=F32)


def _dot_tn(a, b):
    return lax.dot_general(a, b, (((0,), (0,)), ((), ())), preferred_element_type=F32)


def _adaln_kernel(c_ref, w_ref, b_ref, o_ref):
    s = _silu(c_ref[...]).astype(BF16)
    o_ref[0] = _dot(s, w_ref[0].astype(BF16)) + b_ref[0]


def _adaln(cc, ada_w, ada_b):
    depth, d, n3 = ada_w.shape
    tn = 768
    return pl.pallas_call(
        _adaln_kernel,
        grid=(depth, n3 // tn),
        in_specs=[pl.BlockSpec((16, d), lambda i, j: (0, 0)),
                  pl.BlockSpec((1, d, tn), lambda i, j: (i, 0, j)),
                  pl.BlockSpec((1, 1, tn), lambda i, j: (i, 0, j))],
        out_specs=pl.BlockSpec((1, 16, tn), lambda i, j: (i, 0, j)),
        out_shape=jax.ShapeDtypeStruct((depth, 16, n3), F32),
        compiler_params=_cparams(("arbitrary", "arbitrary")),
        name="adaln",
    )(cc, ada_w, ada_b.reshape(depth, 1, n3))


def _mod_row(sub_idx, subs_per_batch):
    b = sub_idx // subs_per_batch
    is_lat = (sub_idx % subs_per_batch != 0).astype(jnp.int32)
    return b * 2 + is_lat


def _inproj_kernel(x_ref, ms_ref, nw_ref, w_ref, ws_ref, z_ref, zs_ref, h_ref, *, n_sub, subs_per_batch):
    i = pl.program_id(0)

    @pl.when(pl.program_id(1) == 0)
    def _():
        for u in range(n_sub):
            m = ms_ref[_mod_row(i * n_sub + u, subs_per_batch)]
            x = x_ref[u * SUB:(u + 1) * SUB, :]
            y = x * lax.rsqrt(jnp.mean(x * x, axis=-1, keepdims=True) + EPS) * nw_ref[...]
            h_ref[u * SUB:(u + 1) * SUB, :] = (y * (1.0 + m[0:1]) + m[1:2]).astype(BF16)
        zs_ref[...] = _dot(h_ref[...], ws_ref[...])

    z_ref[...] = _dot(h_ref[...], w_ref[...]).astype(BF16)


def _inproj(xa, ms, norm_w, w_main, w_small, subs_per_batch):
    bt, d = xa.shape
    tm = 1024 if bt % 1024 == 0 else SUB
    tn = 1280
    return pl.pallas_call(
        functools.partial(_inproj_kernel, n_sub=tm // SUB, subs_per_batch=subs_per_batch),
        grid=(bt // tm, NZ // tn),
        in_specs=[pl.BlockSpec((tm, d), lambda i, j: (i, 0)),
                  pl.BlockSpec(ms.shape, lambda i, j: (0, 0, 0)),
                  pl.BlockSpec((1, d), lambda i, j: (0, 0)),
                  pl.BlockSpec((d, tn), lambda i, j: (0, j)),
                  pl.BlockSpec((d, NZS), lambda i, j: (0, 0))],
        out_specs=[pl.BlockSpec((tm, tn), lambda i, j: (i, j)),
                   pl.BlockSpec((tm, NZS), lambda i, j: (i, 0))],
        out_shape=[jax.ShapeDtypeStruct((bt, NZ), BF16),
                   jax.ShapeDtypeStruct((bt, NZS), F32)],
        scratch_shapes=[pltpu.VMEM((tm, d), BF16)],
        compiler_params=_cparams(("arbitrary", "arbitrary")),
        name="inproj",
    )(xa, ms, norm_w.reshape(1, d), w_main, w_small)


def _outproj_kernel(x_ref, r_ref, g_ref, m_ref, gate_ref, w_ref, fw_ref, o_ref, *, n_sub, subs_per_batch, final):
    i = pl.program_id(0)
    acc = (_dot(r_ref[...], w_ref[0:512, :]) + _dot(g_ref[...], w_ref[512:1024, :])
           + _dot(m_ref[...], w_ref[1024:2048, :]))
    for u in range(n_sub):
        row = (i // (subs_per_batch - 1)) * 2 + 1 if final else _mod_row(i * n_sub + u, subs_per_batch)
        gate = gate_ref[row]
        rows = slice(u * SUB, (u + 1) * SUB)
        xn = x_ref[rows, :] + gate * acc[rows, :]
        if final:
            xn = xn * lax.rsqrt(jnp.mean(xn * xn, axis=-1, keepdims=True) + EPS) * fw_ref[...]
        o_ref[rows, :] = xn


def _outproj(xa, mret, mgdn, mmla, gate_tab, w_out, final_w, subs_per_batch, final):
    bt, d = xa.shape
    if final:
        tm, lat = SUB, subs_per_batch - 1
        n_blocks = (bt // SUB // subs_per_batch) * lat
        src = lambda i: ((i // lat) * subs_per_batch + 1 + i % lat, 0)
        aliases = {}
    else:
        tm = 512 if bt % 512 == 0 else SUB
        n_blocks = bt // tm
        src = lambda i: (i, 0)
        aliases = {0: 0}
    return pl.pallas_call(
        functools.partial(_outproj_kernel, n_sub=tm // SUB, subs_per_batch=subs_per_batch, final=final),
        grid=(n_blocks,),
        in_specs=[pl.BlockSpec((tm, d), src),
                  pl.BlockSpec((tm, 512), src),
                  pl.BlockSpec((tm, 512), src),
                  pl.BlockSpec((tm, 1024), src),
                  pl.BlockSpec(gate_tab.shape, lambda i: (0, 0, 0)),
                  pl.BlockSpec((d, d), lambda i: (0, 0)),
                  pl.BlockSpec((1, d), lambda i: (0, 0))],
        out_specs=pl.BlockSpec((tm, d), lambda i: (i, 0)),
        out_shape=jax.ShapeDtypeStruct((n_blocks * tm, d), F32),
        input_output_aliases=aliases,
        compiler_params=_cparams(("arbitrary",)),
        name="outproj",
    )(xa, mret, mgdn, mmla, gate_tab, w_out, final_w.reshape(1, d))


def _swap_halves(x, half):
    lane = lax.broadcasted_iota(jnp.int32, x.shape, x.ndim - 1)
    first = (lane % (2 * half)) < half
    return jnp.where(first, pltpu.roll(x, 128 - half, x.ndim - 1), pltpu.roll(x, half, x.ndim - 1))


def _rope_tables(n_ctx, seq, dim):
    rows = seq // GRID_W
    row = jnp.repeat(jnp.arange(rows), GRID_W).astype(F32)
    col = jnp.tile(jnp.arange(GRID_W), rows).astype(F32)
    m = dim // 2
    inv = ROPE_BASE ** (-jnp.arange(m // 2, dtype=F32) / (m // 2))
    ar, ac = row[:, None] * inv, col[:, None] * inv
    cos = jnp.concatenate([jnp.cos(ar), jnp.cos(ar), jnp.cos(ac), jnp.cos(ac)], axis=-1)
    sin = jnp.concatenate([-jnp.sin(ar), jnp.sin(ar), -jnp.sin(ac), jnp.sin(ac)], axis=-1)
    cos = jnp.pad(cos, ((n_ctx, 0), (0, 128 - dim)), constant_values=1.0)
    sin = jnp.pad(sin, ((n_ctx, 0), (0, 128 - dim)))
    return cos, sin


def _ret_kernel(dec_ref, q_ref, k_ref, v_ref, g_ref, cos_ref, sin_ref, nw_ref, o_ref,
                acc_ref, qd_ref, inc_ref, st_ref, *, n_chunks):
    h = pl.program_id(1)
    gf = dec_ref[0, h]
    gb = dec_ref[1, h]
    c_len = SUB

    ri = lax.broadcasted_iota(jnp.int32, (c_len, c_len), 0)
    ci = lax.broadcasted_iota(jnp.int32, (c_len, c_len), 1)
    dist = (ri - ci).astype(F32)
    mask = (jnp.where(ri >= ci, jnp.exp(gf * jnp.maximum(dist, 0.0)), 0.0)
            + jnp.where(ri <= ci, jnp.exp(gb * jnp.maximum(-dist, 0.0)), 0.0))
    pos = lax.broadcasted_iota(jnp.int32, (c_len, HD), 0).astype(F32)
    q_dec_f = jnp.exp(gf * (pos + 1.0))
    q_dec_b = jnp.exp(gb * (c_len - pos))
    k_dec_f = jnp.exp(gf * (c_len - 1.0 - pos))
    k_dec_b = jnp.exp(gb * pos)
    s_dec_f = jnp.exp(gf * c_len)
    s_dec_b = jnp.exp(gb * c_len)

    def rows_of(c, n=c_len):
        return pl.ds(c * n if isinstance(c, int) else pl.multiple_of(c * n, n), n)

    grp = math.gcd(n_chunks - 1, 8)

    def for_chunks(fn):
        fn(0)

        def body(g, _):
            for r in range(grp):
                fn(1 + g * grp + r)
            return 0

        lax.fori_loop(0, (n_chunks - 1) // grp, body, 0)

    def local(c):
        rows = rows_of(c)
        cos, sin = cos_ref[rows, :], sin_ref[rows, :]
        q = q_ref[0, rows, :].astype(F32)
        k = k_ref[0, rows, :].astype(F32)
        q = q * cos + _swap_halves(q, 32) * sin
        k = (k * cos + _swap_halves(k, 32) * sin) * (HD ** -0.5)
        v = v_ref[0, rows, :]
        p = (_dot_nt(q.astype(BF16), k.astype(BF16)) * mask).astype(BF16)
        acc_ref[rows, :] = _dot(p, v)
        qd_ref[0, rows, :] = (q * q_dec_f).astype(BF16)
        qd_ref[1, rows, :] = (q * q_dec_b).astype(BF16)
        inc_ref[0, rows_of(c, HD), :] = _dot_tn((k * k_dec_f).astype(BF16), v)
        inc_ref[1, rows_of(c, HD), :] = _dot_tn((k * k_dec_b).astype(BF16), v)

    for_chunks(local)

    def scan(i, carry):
        s_f, s_b = carry
        cb = jnp.where(i == 0, 0, n_chunks - i)
        st_ref[0, rows_of(i, HD), :] = s_f.astype(BF16)
        st_ref[1, rows_of(cb, HD), :] = s_b.astype(BF16)
        return (s_f * s_dec_f + inc_ref[0, rows_of(i, HD), :], s_b * s_dec_b + inc_ref[1, rows_of(cb, HD), :])

    lax.fori_loop(0, n_chunks, scan, (jnp.zeros((HD, HD), F32), jnp.zeros((HD, HD), F32)))

    def finish(c):
        rows = rows_of(c)
        o = (acc_ref[rows, :] + _dot(qd_ref[0, rows, :], st_ref[0, rows_of(c, HD), :])
             + _dot(qd_ref[1, rows, :], st_ref[1, rows_of(c, HD), :]))
        y = o * lax.rsqrt(jnp.mean(o * o, axis=-1, keepdims=True) + EPS) * nw_ref[...]
        o_ref[0, rows, :] = (_silu(g_ref[0, rows, :].astype(F32)) * y).astype(BF16)

    for_chunks(finish)


def _retention(z3, log_decay, norm_w, cos, sin):
    b, t, _ = z3.shape
    cb = COL_RET // HD

    def col(off):
        return pl.BlockSpec((1, t, HD), lambda i, h: (i, 0, cb + off + h))

    return pl.pallas_call(
        functools.partial(_ret_kernel, n_chunks=t // SUB),
        grid=(b, RET_HEADS),
        in_specs=[pl.BlockSpec(memory_space=pltpu.SMEM),
                  col(0), col(4), col(8), col(12),
                  pl.BlockSpec((t, HD), lambda i, h: (0, 0)),
                  pl.BlockSpec((t, HD), lambda i, h: (0, 0)),
                  pl.BlockSpec((1, HD), lambda i, h: (0, 0))],
        out_specs=pl.BlockSpec((1, t, HD), lambda i, h: (i, 0, h)),
        out_shape=jax.ShapeDtypeStruct((b, t, RET_HEADS * HD), BF16),
        scratch_shapes=[pltpu.VMEM((t, HD), F32), pltpu.VMEM((2, t, HD), BF16),
                        pltpu.VMEM((2, (t // SUB) * HD, HD), F32), pltpu.VMEM((2, (t // SUB) * HD, HD), BF16)],
        compiler_params=_cparams(("arbitrary", "arbitrary")),
        name="retention",
    )(log_decay, z3, z3, z3, z3, cos, sin, norm_w.reshape(1, HD))


LANE_GC = 64
LANE_BETA = 72


def _gdn_gate_kernel(zs_ref, alog_ref, dtb_ref, col_ref, row_ref):
    x = zs_ref[...]
    z = x + dtb_ref[...]
    softplus = jnp.maximum(z, 0.0) + jnp.log(1.0 + jnp.exp(-jnp.abs(z)))
    g = -jnp.exp(alog_ref[...]) * softplus
    c = GDN_CHUNK
    pos = lax.broadcasted_iota(jnp.int32, (SUB, NZS), 0) % c
    lane = lax.broadcasted_iota(jnp.int32, (SUB, NZS), 1)
    pre = suf = g
    sh = 1
    while sh < c:
        pre = pre + jnp.where(pos >= sh, pltpu.roll(pre, sh, 0), 0.0)
        suf = suf + jnp.where(pos < c - sh, pltpu.roll(suf, SUB - sh, 0), 0.0)
        sh *= 2
    beta = 1.0 / (1.0 + jnp.exp(-x))
    cols = jnp.where(lane < LANE_GC + GDN_HEADS, pre, jnp.where(lane < LANE_BETA, suf, beta))
    col_ref[...] = cols
    row_ref[0] = cols.T


def _gdn_gates(zs, alog_lane, dtb_lane, b, t):
    spb = t // SUB
    return pl.pallas_call(
        _gdn_gate_kernel,
        grid=(b * spb,),
        in_specs=[pl.BlockSpec((SUB, NZS), lambda i: (i, 0)),
                  pl.BlockSpec((1, NZS), lambda i: (0, 0)),
                  pl.BlockSpec((1, NZS), lambda i: (0, 0))],
        out_specs=[pl.BlockSpec((SUB, NZS), lambda i: (i, 0)),
                   pl.BlockSpec((1, NZS, SUB), lambda i: (i // spb, 0, i % spb))],
        out_shape=[jax.ShapeDtypeStruct((b * t, NZS), F32),
                   jax.ShapeDtypeStruct((b, NZS, t), F32)],
        compiler_params=_cparams(("arbitrary",)),
        name="gdn_gates",
    )(zs, alog_lane, dtb_lane)


GDN_GROUP = 34


def _gdn_kernel(q_ref, k_ref, v_ref, g_ref, cwq_ref, cwk_ref, cwv_ref, col_ref, row_ref, nw_ref, o_ref,
                pad_ref, q32_ref, k32_ref, v32_ref, w_ref, u_ref, qe_ref, sq_ref, sm_ref, att_ref, eg_ref, acc_ref,
                *, n_tok):
    head = pl.program_id(1)
    c_len = GDN_CHUNK
    n_chunks = n_tok // c_len
    n_sub = n_tok // SUB
    grp = GDN_GROUP if n_chunks % GDN_GROUP == 0 else SUB // c_len
    halo = 8

    def pad_row(s):
        return pl.multiple_of(halo + s * SUB + halo * jnp.minimum(s, 1), 8)

    for gap in (0, halo + SUB, 2 * halo + n_tok):
        pad_ref[gap:gap + halo, :] = jnp.zeros((halo, HD), F32)

    for src_ref, cw_ref, dst_ref, norm_scale in ((q_ref, cwq_ref, q32_ref, HD ** -0.5),
                                                 (k_ref, cwk_ref, k32_ref, 1.0),
                                                 (v_ref, cwv_ref, v32_ref, None)):
        def fill(s, _):
            rows = pl.ds(pl.multiple_of(s * SUB, SUB), SUB)
            pad_ref[pl.ds(pad_row(s), SUB), :] = src_ref[0, rows, :].astype(F32)
            return 0

        lax.fori_loop(0, n_sub, fill, 0)
        cw = cw_ref[...]

        def conv(s, _):
            base = pl.multiple_of(pad_row(s) - halo, 8)
            y = None
            for j in range(GDN_CONV):
                tap = pad_ref[pl.ds(base + (halo + j - GDN_CONV // 2), SUB), :] * cw[j:j + 1, :]
                y = tap if y is None else y + tap
            y = _silu(y)
            if norm_scale is not None:
                y = y * (lax.rsqrt(jnp.sum(y * y, axis=-1, keepdims=True) + EPS) * norm_scale)
            dst_ref[pl.ds(pl.multiple_of(s * SUB, SUB), SUB), :] = y
            return 0

        lax.fori_loop(0, n_sub, conv, 0)

    ri = lax.broadcasted_iota(jnp.int32, (grp, c_len, c_len), 1)
    ci = lax.broadcasted_iota(jnp.int32, (grp, c_len, c_len), 2)
    eye = (ri == ci).astype(F32)
    tri = (ri >= ci, ri <= ci)
    strict = (ri > ci, ri < ci)
    merge_masks = []
    m = 1
    while m < c_len:
        merge_masks.append(((ri // (2 * m)) == (ci // (2 * m))) & ((ri // m) != (ci // m)))
        m *= 2

    def bmm(a, b):
        return lax.dot_general(a, b, (((2,), (1,)), ((0,), (0,))), preferred_element_type=F32)

    def bmm_nt(a, b):
        return lax.dot_general(a, b, (((2,), (2,)), ((0,), (0,))), preferred_element_type=F32)

    def bmm_tn(a, b):
        return lax.dot_general(a, b, (((1,), (1,)), ((0,), (0,))), preferred_element_type=F32)

    def local(gi, _):
        r0 = pl.multiple_of(gi * (grp * c_len), grp * c_len)
        rows = pl.ds(r0, grp * c_len)
        q = q32_ref[rows, :].reshape(grp, c_len, HD)
        k = k32_ref[rows, :].reshape(grp, c_len, HD)
        v = v32_ref[rows, :].reshape(grp, c_len, HD)
        kb16 = k.astype(BF16)
        qkk = bmm_nt(jnp.concatenate([q.astype(BF16), kb16], axis=1), kb16)
        qk, kk = qkk[:, :c_len], qkk[:, c_len:]
        cols = col_ref[0, rows, :]
        col_lane = lax.broadcasted_iota(jnp.int32, cols.shape, 1)

        def pick(lane0):
            sel = jnp.where(col_lane == lane0 + head, cols, 0.0)
            return jnp.sum(sel, axis=-1, keepdims=True).reshape(grp, c_len, 1)

        for d in range(2):
            gc = pick(LANE_GC + GDN_HEADS * d)
            beta = pick(LANE_BETA + GDN_HEADS * d)
            gc_row = jnp.stack([row_ref[0, d, 0, pl.ds(gi * grp + g, 1), :] for g in range(grp)], axis=0)
            g_tot = gc[:, c_len - 1:c_len, :] if d == 0 else gc[:, 0:1, :]
            decay = jnp.exp(jnp.where(tri[d], gc - gc_row, -jnp.inf))
            kbeta = k * beta
            a = jnp.where(strict[d], kk * beta * decay, 0.0)
            t = eye - jnp.where(merge_masks[0], a, 0.0)
            for mm in merge_masks[1:]:
                t16 = t.astype(BF16)
                t = t - bmm(bmm(t16, jnp.where(mm, a, 0.0).astype(BF16)).astype(BF16), t16)
            rhs = jnp.concatenate([v * beta, kbeta * jnp.exp(gc)], axis=-1).astype(BF16)
            sol = bmm(t.astype(BF16), rhs)
            u, w16 = sol[:, :, :HD], sol[:, :, HD:].astype(BF16)
            u_ref[d, rows, :] = u.reshape(grp * c_len, HD)
            w_ref[d, rows, :] = w16.reshape(grp * c_len, HD)
            qe_ref[d, rows, :] = (q * jnp.exp(gc)).reshape(grp * c_len, HD).astype(BF16)
            att_ref[d, rows, :] = (qk * decay).reshape(grp * c_len, c_len).astype(BF16)
            kd = (k * jnp.exp(g_tot - gc)).astype(BF16)
            srows = pl.ds(pl.multiple_of(gi * (grp * HD), grp * HD), grp * HD)
            sq_ref[d, srows, :] = bmm_tn(kd, u.astype(BF16)).reshape(grp * HD, HD).astype(BF16)
            sm_ref[d, srows, :] = bmm_tn(kd, w16).reshape(grp * HD, HD).astype(BF16)
            eg_ref[d, pl.ds(pl.multiple_of(gi * grp, grp), grp), :] = jnp.broadcast_to(
                jnp.exp(g_tot).reshape(grp, 1), (grp, HD))
        return 0

    lax.fori_loop(0, n_chunks // grp, local, 0)

    ctx_chunks = SUB // c_len

    def step(d, c, s):
        rows = pl.ds(pl.multiple_of(c * c_len, c_len), c_len)
        srows = pl.ds(pl.multiple_of(c * HD, HD), HD)
        s16 = s.astype(BF16)
        v_new = (u_ref[d, rows, :] - _dot(w_ref[d, rows, :], s16)).astype(BF16)
        acc_ref[d, rows, :] = _dot(qe_ref[d, rows, :], s16) + _dot(att_ref[d, rows, :], v_new)
        return s * eg_ref[d, pl.ds(c, 1), :] + sq_ref[d, srows, :].astype(F32) - _dot(sm_ref[d, srows, :], s16)

    s0 = jnp.zeros((HD, HD), F32)
    scan_unroll = grp

    def scan(i, carry):
        s_f, s_b = carry
        for r in range(scan_unroll):
            j = i * scan_unroll + r
            s_f = step(0, j, s_f)
            s_b = step(1, jnp.where(j < ctx_chunks, ctx_chunks - 1 - j, n_chunks - 1 + ctx_chunks - j), s_b)
        return s_f, s_b

    lax.fori_loop(0, n_chunks // scan_unroll, scan, (s0, s0))

    def finish(s, _):
        rows = pl.ds(pl.multiple_of(s * SUB, SUB), SUB)
        o = acc_ref[0, rows, :] + acc_ref[1, rows, :]
        y = o * lax.rsqrt(jnp.mean(o * o, axis=-1, keepdims=True) + EPS) * nw_ref[...]
        o_ref[0, rows, :] = (_silu(g_ref[0, rows, :].astype(F32)) * y).astype(BF16)
        return 0

    lax.fori_loop(0, n_sub, finish, 0)


def _gdn(z3, conv_w, cols, rows, norm_w):
    b, t, _ = z3.shape
    cb = COL_GDN // HD
    n_chunks = t // GDN_CHUNK

    once = pl.Buffered(1)

    def col(off):
        return pl.BlockSpec((1, t, HD), lambda i, h: (i, 0, cb + off + h), pipeline_mode=once)

    def cw(off):
        return pl.BlockSpec((GDN_CONV, HD), lambda i, h: (0, off + h))

    return pl.pallas_call(
        functools.partial(_gdn_kernel, n_tok=t),
        grid=(b, GDN_HEADS),
        in_specs=[col(0), col(4), col(8), col(12), cw(0), cw(4), cw(8),
                  pl.BlockSpec((1, t, NZS), lambda i, h: (i, 0, 0), pipeline_mode=once),
                  pl.BlockSpec((1, 2, 1, n_chunks, GDN_CHUNK), lambda i, h: (i, 0, h, 0, 0)),
                  pl.BlockSpec((1, HD), lambda i, h: (0, 0))],
        out_specs=pl.BlockSpec((1, t, HD), lambda i, h: (i, 0, h)),
        out_shape=jax.ShapeDtypeStruct((b, t, GDN_HEADS * HD), BF16),
        scratch_shapes=[pltpu.VMEM((t + 24, HD), F32),
                        pltpu.VMEM((t, HD), F32), pltpu.VMEM((t, HD), F32), pltpu.VMEM((t, HD), F32),
                        pltpu.VMEM((2, t, HD), BF16), pltpu.VMEM((2, t, HD), F32),
                        pltpu.VMEM((2, t, HD), BF16),
                        pltpu.VMEM((2, n_chunks * HD, HD), BF16), pltpu.VMEM((2, n_chunks * HD, HD), BF16),
                        pltpu.VMEM((2, t, GDN_CHUNK), BF16),
                        pltpu.VMEM((2, n_chunks, HD), F32),
                        pltpu.VMEM((2, t, HD), F32)],
        compiler_params=_cparams(("arbitrary", "arbitrary")),
        name="gdn",
    )(z3, z3, z3, z3, conv_w, conv_w, conv_w, cols, rows, norm_w.reshape(1, HD))


def _mla_prep_kernel(cq_ref, ckv_ref, zs_ref, qnw_ref, kvnw_ref, wq_ref, wkv_ref, cos_ref, sin_ref,
                     q_ref, k_ref, v_ref):
    def rms(x, w):
        return (x * lax.rsqrt(jnp.mean(x * x, axis=-1, keepdims=True) + EPS) * w).astype(BF16)

    cos, sin = cos_ref[...], sin_ref[...]

    def rope(x):
        return x * cos + _swap_halves(x, 16) * sin

    q = _dot(rms(cq_ref[...].astype(F32), qnw_ref[...]), wq_ref[...])
    kv = _dot(rms(ckv_ref[...].astype(F32), kvnw_ref[...]), wkv_ref[...])
    lane = lax.broadcasted_iota(jnp.int32, (SUB, HD), 1)
    kr = rope(jnp.where(lane < MLA_ROPE, zs_ref[...], 0.0)).astype(BF16)
    for h in range(MLA_HEADS):
        base = h * MLA_QK
        q_ref[0, h, :, 0:HD] = (q[:, base:base + HD] * MLA_Q_SCALE).astype(BF16)
        q_ref[0, h, :, HD:MLA_QK] = (rope(q[:, base + HD:base + MLA_QK]) * MLA_Q_SCALE).astype(BF16)
        k_ref[0, h, :, 0:HD] = kv[:, h * HD:(h + 1) * HD].astype(BF16)
        k_ref[0, h, :, HD:MLA_QK] = kr
        v_ref[0, h] = kv[:, (MLA_HEADS + h) * HD:(MLA_HEADS + h + 1) * HD].astype(BF16)


def _mla_prep(z, zs, qnw, kvnw, wq, wkv, cos, sin, b, t):
    spb = t // SUB

    def tok(width, blk):
        return pl.BlockSpec((SUB, width), lambda i: (i, blk))

    def par(shape):
        return pl.BlockSpec(shape, lambda i: (0, 0))

    def head_out(width):
        return pl.BlockSpec((1, MLA_HEADS, SUB, width), lambda i: (i // spb, 0, i % spb, 0))

    return pl.pallas_call(
        _mla_prep_kernel,
        grid=(b * spb,),
        in_specs=[tok(MLA_Q_RANK, COL_CQ // MLA_Q_RANK), tok(MLA_KV_RANK, COL_CKV // MLA_KV_RANK), tok(NZS, 0),
                  par((1, MLA_Q_RANK)), par((1, MLA_KV_RANK)),
                  par(wq.shape), par(wkv.shape),
                  pl.BlockSpec((SUB, HD), lambda i: (i % spb, 0)),
                  pl.BlockSpec((SUB, HD), lambda i: (i % spb, 0))],
        out_specs=[head_out(MLA_QK), head_out(MLA_QK), head_out(HD)],
        out_shape=[jax.ShapeDtypeStruct((b, MLA_HEADS, t, MLA_QK), BF16),
                   jax.ShapeDtypeStruct((b, MLA_HEADS, t, MLA_QK), BF16),
                   jax.ShapeDtypeStruct((b, MLA_HEADS, t, HD), BF16)],
        compiler_params=_cparams(("arbitrary",)),
        name="mla_prep",
    )(z, z, zs, qnw.reshape(1, -1), kvnw.reshape(1, -1), wq, wkv, cos, sin)


MLA_HEADS_PER_STEP = 8


def _mla_attn_kernel(q_ref, k_ref, v_ref, *gate_and_out_refs):
    g_refs, o_ref = gate_and_out_refs[:-1], gate_and_out_refs[-1]

    def attend(n_keys):
        for hh in range(MLA_HEADS_PER_STEP):
            s = _dot_nt(q_ref[0, hh], k_ref[0, hh, 0:n_keys, :])
            p = jnp.exp2(s - jnp.max(s, axis=-1, keepdims=True))
            l = jnp.sum(p, axis=-1, keepdims=True)
            o = _dot(p.astype(BF16), v_ref[0, hh, 0:n_keys, :]) / l
            gate = _silu(g_refs[hh // 2][0, :, (hh % 2) * HD:(hh % 2 + 1) * HD].astype(F32))
            o_ref[0, :, hh * HD:(hh + 1) * HD] = (gate * o).astype(BF16)

    @pl.when(pl.program_id(2) == 0)
    def _():
        attend(SUB)

    @pl.when(pl.program_id(2) > 0)
    def _():
        attend(k_ref.shape[2])


def _mla_attn(qp, kp, vp, z3):
    b, nh, t, _ = qp.shape
    hs = MLA_HEADS_PER_STEP
    assert hs % 2 == 0 and nh % hs == 0 and COL_MGATE % (2 * HD) == 0
    gb = COL_MGATE // (2 * HD)
    once = pl.Buffered(1)

    def gate_spec(pair):
        return pl.BlockSpec((1, SUB, 2 * HD), lambda i, h, j: (i, j, gb + (hs // 2) * h + pair))

    return pl.pallas_call(
        _mla_attn_kernel,
        grid=(b, nh // hs, t // SUB),
        in_specs=[pl.BlockSpec((1, hs, SUB, MLA_QK), lambda i, h, j: (i, h, j, 0)),
                  pl.BlockSpec((1, hs, t, MLA_QK), lambda i, h, j: (i, h, 0, 0), pipeline_mode=once),
                  pl.BlockSpec((1, hs, t, HD), lambda i, h, j: (i, h, 0, 0), pipeline_mode=once)]
                 + [gate_spec(pair) for pair in range(hs // 2)],
        out_specs=pl.BlockSpec((1, SUB, hs * HD), lambda i, h, j: (i, j, h)),
        out_shape=jax.ShapeDtypeStruct((b, t, nh * HD), BF16),
        compiler_params=_cparams(("arbitrary", "arbitrary", "arbitrary")),
        name="mla_attn",
    )(qp, kp, vp, *([z3] * (hs // 2)))


def _layout_w_in(w_in):
    sizes = (512, 512, 512, 512, 512, 512, 512, 512, 8, 8, MLA_Q_RANK, MLA_KV_RANK, MLA_ROPE, 1024)
    offs = np.concatenate([[0], np.cumsum(sizes)])
    part = lambda n: w_in[:, offs[n]:offs[n + 1]]
    w_main = jnp.concatenate([part(n) for n in (0, 1, 2, 3, 4, 5, 6, 7, 11, 10, 13)], axis=-1)
    w_small = jnp.concatenate([part(12), part(8), part(9)], axis=-1)
    w_small = jnp.pad(w_small, ((0, 0), (0, NZS - w_small.shape[-1])))
    return w_main.astype(BF16), w_small.astype(BF16)


def _layout_w_qb(w_qb):
    w = w_qb.reshape(MLA_Q_RANK, MLA_HEADS, HD + MLA_ROPE)
    w = jnp.pad(w, ((0, 0), (0, 0), (0, MLA_QK - HD - MLA_ROPE)))
    return w.reshape(MLA_Q_RANK, MLA_HEADS * MLA_QK).astype(BF16)


def _layout_w_kvb(w_kvb):
    w = w_kvb.reshape(MLA_KV_RANK, MLA_HEADS, 2, HD)
    return jnp.transpose(w, (0, 2, 1, 3)).reshape(MLA_KV_RANK, 2 * MLA_HEADS * HD).astype(BF16)


def kernel(x, c, ctx, c_ctx, ada_w, ada_b, norm_w, w_in, ret_log_decay, ret_norm_w, gdn_conv_w, gdn_a_log,
           gdn_dt_bias, gdn_norm_w, mla_q_norm_w, mla_w_qb, mla_kv_norm_w, mla_w_kvb, w_out, final_norm_w):
    b, seq, d = x.shape
    n_ctx = ctx.shape[1]
    assert n_ctx == SUB and seq % SUB == 0 and d == D_MODEL
    depth = ada_w.shape[0]
    t = n_ctx + seq
    spb = t // SUB
    n_chunks = t // GDN_CHUNK

    cc = jnp.zeros((16, d), F32).at[:b].set(c).at[b].set(c_ctx)
    mod = _adaln(cc, ada_w, ada_b)
    sel = np.stack([np.full(b, b), np.arange(b)], axis=1).reshape(-1)

    ret_cos, ret_sin = _rope_tables(n_ctx, seq, HD)
    mla_cos, mla_sin = _rope_tables(n_ctx, seq, MLA_ROPE)

    xa = jnp.concatenate([ctx, x], axis=1).reshape(b * t, d)
    for i in range(depth):
        final = i == depth - 1
        shift, scale, gate = mod[i, :, :d], mod[i, :, d:2 * d], mod[i, :, 2 * d:]
        ms = jnp.stack([scale[sel], shift[sel]], axis=1)
        gate_tab = gate[sel][:, None, :]
        w_main, w_small = _layout_w_in(w_in[i])

        z, zs = _inproj(xa, ms, norm_w[i], w_main, w_small, spb)
        z3 = z.reshape(b, t, NZ)

        m_ret = _retention(z3, ret_log_decay[i], ret_norm_w[i], ret_cos, ret_sin)

        on_lanes = lambda p: jnp.zeros((1, NZS), F32).at[0, LANE_GC:LANE_GC + 2 * GDN_HEADS].set(p.reshape(-1))
        cols, rows_t = _gdn_gates(zs, on_lanes(gdn_a_log[i]), on_lanes(gdn_dt_bias[i]), b, t)
        gc_rows = rows_t[:, LANE_GC:LANE_GC + 2 * GDN_HEADS].reshape(b, 2, GDN_HEADS, n_chunks, GDN_CHUNK)
        m_gdn = _gdn(z3, gdn_conv_w[i], cols.reshape(b, t, NZS), gc_rows, gdn_norm_w[i])

        qp, kp, vp = _mla_prep(z, zs, mla_q_norm_w[i], mla_kv_norm_w[i], _layout_w_qb(mla_w_qb[i]),
                               _layout_w_kvb(mla_w_kvb[i]), mla_cos, mla_sin, b, t)
        m_mla = _mla_attn(qp, kp, vp, z3)

        xa = _outproj(xa, m_ret.reshape(b * t, -1), m_gdn.reshape(b * t, -1), m_mla.reshape(b * t, -1),
                      gate_tab, w_out[i].astype(BF16), final_norm_w, spb, final)
    return xa.reshape(b, seq, d)
```

```python
import functools
import math

import jax
import jax.numpy as jnp
import numpy as np
from jax import lax
from jax.experimental import pallas as pl
from jax.experimental.pallas import tpu as pltpu

F32 = jnp.float32
BF16 = jnp.bfloat16

EPS = 1e-6
ROPE_BASE = 10000.0
GRID_W = 64

D_MODEL = 2048
SUB = 256
RET_HEADS = 4
GDN_HEADS = 4
HD = 128
GDN_CONV = 5
GDN_CHUNK = 64
MLA_HEADS = 8
MLA_ROPE = 64
MLA_Q_RANK = 768
MLA_KV_RANK = 512
MLA_QK = 256
MLA_SCALE = (HD + MLA_ROPE) ** -0.5
MLA_Q_SCALE = MLA_SCALE * math.log2(math.e)

COL_RET = 0
COL_GDN = 2048
COL_CKV = 4096
COL_CQ = 4608
COL_MGATE = 5376
NZ = 6400
NZS = 128

VMEM_LIMIT = 56 * 1024 * 1024


def _cparams(sem):
    return pltpu.CompilerParams(dimension_semantics=sem, vmem_limit_bytes=VMEM_LIMIT)


def _silu(x):
    return x * (1.0 / (1.0 + jnp.exp(-x)))


def _dot(a, b):
    return jnp.dot(a, b, preferred_element_type=F32)


def _dot_nt(a, b):
    return lax.dot_general(a, b, (((1,), (1,)), ((), ())), preferred_element_type=F32)


def _dot_tn(a, b):
    return lax.dot_general(a, b, (((0,), (0,)), ((), ())), preferred_element_type=F32)


def _adaln_kernel(c_ref, w_ref, b_ref, o_ref):
    s = _silu(c_ref[...]).astype(BF16)
    o_ref[0] = _dot(s, w_ref[0].astype(BF16)) + b_ref[0]


def _adaln(cc, ada_w, ada_b):
    depth, d, n3 = ada_w.shape
    tn = 768
    return pl.pallas_call(
        _adaln_kernel,
        grid=(depth, n3 // tn),
        in_specs=[pl.BlockSpec((16, d), lambda i, j: (0, 0)),
                  pl.BlockSpec((1, d, tn), lambda i, j: (i, 0, j)),
                  pl.BlockSpec((1, 1, tn), lambda i, j: (i, 0, j))],
        out_specs=pl.BlockSpec((1, 16, tn), lambda i, j: (i, 0, j)),
        out_shape=jax.ShapeDtypeStruct((depth, 16, n3), F32),
        compiler_params=_cparams(("arbitrary", "arbitrary")),
        name="adaln",
    )(cc, ada_w, ada_b.reshape(depth, 1, n3))


def _mod_row(sub_idx, subs_per_batch):
    b = sub_idx // subs_per_batch
    is_lat = (sub_idx % subs_per_batch != 0).astype(jnp.int32)
    return b * 2 + is_lat


def _inproj_kernel(x_ref, ms_ref, nw_ref, w_ref, ws_ref, z_ref, zs_ref, h_ref, *, n_sub, subs_per_batch):
    i = pl.program_id(0)

    @pl.when(pl.program_id(1) == 0)
    def _():
        for u in range(n_sub):
            m = ms_ref[_mod_row(i * n_sub + u, subs_per_batch)]
            x = x_ref[u * SUB:(u + 1) * SUB, :]
            y = x * lax.rsqrt(jnp.mean(x * x, axis=-1, keepdims=True) + EPS) * nw_ref[...]
            h_ref[u * SUB:(u + 1) * SUB, :] = (y * (1.0 + m[0:1]) + m[1:2]).astype(BF16)
        zs_ref[...] = _dot(h_ref[...], ws_ref[...])

    z_ref[...] = _dot(h_ref[...], w_ref[...]).astype(BF16)


def _inproj(xa, ms, norm_w, w_main, w_small, subs_per_batch):
    bt, d = xa.shape
    tm = 1024 if bt % 1024 == 0 else SUB
    tn = 1280
    return pl.pallas_call(
        functools.partial(_inproj_kernel, n_sub=tm // SUB, subs_per_batch=subs_per_batch),
        grid=(bt // tm, NZ // tn),
        in_specs=[pl.BlockSpec((tm, d), lambda i, j: (i, 0)),
                  pl.BlockSpec(ms.shape, lambda i, j: (0, 0, 0)),
                  pl.BlockSpec((1, d), lambda i, j: (0, 0)),
                  pl.BlockSpec((d, tn), lambda i, j: (0, j)),
                  pl.BlockSpec((d, NZS), lambda i, j: (0, 0))],
        out_specs=[pl.BlockSpec((tm, tn), lambda i, j: (i, j)),
                   pl.BlockSpec((tm, NZS), lambda i, j: (i, 0))],
        out_shape=[jax.ShapeDtypeStruct((bt, NZ), BF16),
                   jax.ShapeDtypeStruct((bt, NZS), F32)],
        scratch_shapes=[pltpu.VMEM((tm, d), BF16)],
        compiler_params=_cparams(("arbitrary", "arbitrary")),
        name="inproj",
    )(xa, ms, norm_w.reshape(1, d), w_main, w_small)


def _outproj_kernel(x_ref, r_ref, g_ref, m_ref, gate_ref, w_ref, fw_ref, o_ref, *, n_sub, subs_per_batch, final):
    i = pl.program_id(0)
    acc = (_dot(r_ref[...], w_ref[0:512, :]) + _dot(g_ref[...], w_ref[512:1024, :])
           + _dot(m_ref[...], w_ref[1024:2048, :]))
    for u in range(n_sub):
        row = (i // (subs_per_batch - 1)) * 2 + 1 if final else _mod_row(i * n_sub + u, subs_per_batch)
        gate = gate_ref[row]
        rows = slice(u * SUB, (u + 1) * SUB)
        xn = x_ref[rows, :] + gate * acc[rows, :]
        if final:
            xn = xn * lax.rsqrt(jnp.mean(xn * xn, axis=-1, keepdims=True) + EPS) * fw_ref[...]
        o_ref[rows, :] = xn


def _outproj(xa, mret, mgdn, mmla, gate_tab, w_out, final_w, subs_per_batch, final):
    bt, d = xa.shape
    if final:
        tm, lat = SUB, subs_per_batch - 1
        n_blocks = (bt // SUB // subs_per_batch) * lat
        src = lambda i: ((i // lat) * subs_per_batch + 1 + i % lat, 0)
        aliases = {}
    else:
        tm = 512 if bt % 512 == 0 else SUB
        n_blocks = bt // tm
        src = lambda i: (i, 0)
        aliases = {0: 0}
    return pl.pallas_call(
        functools.partial(_outproj_kernel, n_sub=tm // SUB, subs_per_batch=subs_per_batch, final=final),
        grid=(n_blocks,),
        in_specs=[pl.BlockSpec((tm, d), src),
                  pl.BlockSpec((tm, 512), src),
                  pl.BlockSpec((tm, 512), src),
                  pl.BlockSpec((tm, 1024), src),
                  pl.BlockSpec(gate_tab.shape, lambda i: (0, 0, 0)),
                  pl.BlockSpec((d, d), lambda i: (0, 0)),
                  pl.BlockSpec((1, d), lambda i: (0, 0))],
        out_specs=pl.BlockSpec((tm, d), lambda i: (i, 0)),
        out_shape=jax.ShapeDtypeStruct((n_blocks * tm, d), F32),
        input_output_aliases=aliases,
        compiler_params=_cparams(("arbitrary",)),
        name="outproj",
    )(xa, mret, mgdn, mmla, gate_tab, w_out, final_w.reshape(1, d))


def _swap_halves(x, half):
    lane = lax.broadcasted_iota(jnp.int32, x.shape, x.ndim - 1)
    first = (lane % (2 * half)) < half
    return jnp.where(first, pltpu.roll(x, 128 - half, x.ndim - 1), pltpu.roll(x, half, x.ndim - 1))


def _rope_tables(n_ctx, seq, dim):
    rows = seq // GRID_W
    row = jnp.repeat(jnp.arange(rows), GRID_W).astype(F32)
    col = jnp.tile(jnp.arange(GRID_W), rows).astype(F32)
    m = dim // 2
    inv = ROPE_BASE ** (-jnp.arange(m // 2, dtype=F32) / (m // 2))
    ar, ac = row[:, None] * inv, col[:, None] * inv
    cos = jnp.concatenate([jnp.cos(ar), jnp.cos(ar), jnp.cos(ac), jnp.cos(ac)], axis=-1)
    sin = jnp.concatenate([-jnp.sin(ar), jnp.sin(ar), -jnp.sin(ac), jnp.sin(ac)], axis=-1)
    cos = jnp.pad(cos, ((n_ctx, 0), (0, 128 - dim)), constant_values=1.0)
    sin = jnp.pad(sin, ((n_ctx, 0), (0, 128 - dim)))
    return cos, sin


def _ret_kernel(dec_ref, q_ref, k_ref, v_ref, g_ref, cos_ref, sin_ref, nw_ref, o_ref,
                acc_ref, qd_ref, inc_ref, st_ref, *, n_chunks):
    h = pl.program_id(1)
    gf = dec_ref[0, h]
    gb = dec_ref[1, h]
    c_len = SUB

    ri = lax.broadcasted_iota(jnp.int32, (c_len, c_len), 0)
    ci = lax.broadcasted_iota(jnp.int32, (c_len, c_len), 1)
    dist = (ri - ci).astype(F32)
    mask = (jnp.where(ri >= ci, jnp.exp(gf * jnp.maximum(dist, 0.0)), 0.0)
            + jnp.where(ri <= ci, jnp.exp(gb * jnp.maximum(-dist, 0.0)), 0.0))
    pos = lax.broadcasted_iota(jnp.int32, (c_len, HD), 0).astype(F32)
    q_dec_f = jnp.exp(gf * (pos + 1.0))
    q_dec_b = jnp.exp(gb * (c_len - pos))
    k_dec_f = jnp.exp(gf * (c_len - 1.0 - pos))
    k_dec_b = jnp.exp(gb * pos)
    s_dec_f = jnp.exp(gf * c_len)
    s_dec_b = jnp.exp(gb * c_len)

    def rows_of(c, n=c_len):
        return pl.ds(c * n if isinstance(c, int) else pl.multiple_of(c * n, n), n)

    grp = math.gcd(n_chunks - 1, 8)

    def for_chunks(fn):
        fn(0)

        def body(g, _):
            for r in range(grp):
                fn(1 + g * grp + r)
            return 0

        lax.fori_loop(0, (n_chunks - 1) // grp, body, 0)

    def local(c):
        rows = rows_of(c)
        cos, sin = cos_ref[rows, :], sin_ref[rows, :]
        q = q_ref[0, rows, :].astype(F32)
        k = k_ref[0, rows, :].astype(F32)
        q = q * cos + _swap_halves(q, 32) * sin
        k = (k * cos + _swap_halves(k, 32) * sin) * (HD ** -0.5)
        v = v_ref[0, rows, :]
        p = (_dot_nt(q.astype(BF16), k.astype(BF16)) * mask).astype(BF16)
        acc_ref[rows, :] = _dot(p, v)
        qd_ref[0, rows, :] = (q * q_dec_f).astype(BF16)
        qd_ref[1, rows, :] = (q * q_dec_b).astype(BF16)
        inc_ref[0, rows_of(c, HD), :] = _dot_tn((k * k_dec_f).astype(BF16), v)
        inc_ref[1, rows_of(c, HD), :] = _dot_tn((k * k_dec_b).astype(BF16), v)

    for_chunks(local)

    def scan(i, carry):
        s_f, s_b = carry
        cb = jnp.where(i == 0, 0, n_chunks - i)
        st_ref[0, rows_of(i, HD), :] = s_f.astype(BF16)
        st_ref[1, rows_of(cb, HD), :] = s_b.astype(BF16)
        return (s_f * s_dec_f + inc_ref[0, rows_of(i, HD), :], s_b * s_dec_b + inc_ref[1, rows_of(cb, HD), :])

    lax.fori_loop(0, n_chunks, scan, (jnp.zeros((HD, HD), F32), jnp.zeros((HD, HD), F32)))

    def finish(c):
        rows = rows_of(c)
        o = (acc_ref[rows, :] + _dot(qd_ref[0, rows, :], st_ref[0, rows_of(c, HD), :])
             + _dot(qd_ref[1, rows, :], st_ref[1, rows_of(c, HD), :]))
        y = o * lax.rsqrt(jnp.mean(o * o, axis=-1, keepdims=True) + EPS) * nw_ref[...]
        o_ref[0, rows, :] = (_silu(g_ref[0, rows, :].astype(F32)) * y).astype(BF16)

    for_chunks(finish)


def _retention(z3, log_decay, norm_w, cos, sin):
    b, t, _ = z3.shape
    cb = COL_RET // HD

    def col(off):
        return pl.BlockSpec((1, t, HD), lambda i, h: (i, 0, cb + off + h))

    return pl.pallas_call(
        functools.partial(_ret_kernel, n_chunks=t // SUB),
        grid=(b, RET_HEADS),
        in_specs=[pl.BlockSpec(memory_space=pltpu.SMEM),
                  col(0), col(4), col(8), col(12),
                  pl.BlockSpec((t, HD), lambda i, h: (0, 0)),
                  pl.BlockSpec((t, HD), lambda i, h: (0, 0)),
                  pl.BlockSpec((1, HD), lambda i, h: (0, 0))],
        out_specs=pl.BlockSpec((1, t, HD), lambda i, h: (i, 0, h)),
        out_shape=jax.ShapeDtypeStruct((b, t, RET_HEADS * HD), BF16),
        scratch_shapes=[pltpu.VMEM((t, HD), F32), pltpu.VMEM((2, t, HD), BF16),
                        pltpu.VMEM((2, (t // SUB) * HD, HD), F32), pltpu.VMEM((2, (t // SUB) * HD, HD), BF16)],
        compiler_params=_cparams(("arbitrary", "arbitrary")),
        name="retention",
    )(log_decay, z3, z3, z3, z3, cos, sin, norm_w.reshape(1, HD))


LANE_GC = 64
LANE_BETA = 72


def _gdn_gate_kernel(zs_ref, alog_ref, dtb_ref, col_ref, row_ref):
    x = zs_ref[...]
    z = x + dtb_ref[...]
    softplus = jnp.maximum(z, 0.0) + jnp.log(1.0 + jnp.exp(-jnp.abs(z)))
    g = -jnp.exp(alog_ref[...]) * softplus
    c = GDN_CHUNK
    pos = lax.broadcasted_iota(jnp.int32, (SUB, NZS), 0) % c
    lane = lax.broadcasted_iota(jnp.int32, (SUB, NZS), 1)
    pre = suf = g
    sh = 1
    while sh < c:
        pre = pre + jnp.where(pos >= sh, pltpu.roll(pre, sh, 0), 0.0)
        suf = suf + jnp.where(pos < c - sh, pltpu.roll(suf, SUB - sh, 0), 0.0)
        sh *= 2
    beta = 1.0 / (1.0 + jnp.exp(-x))
    cols = jnp.where(lane < LANE_GC + GDN_HEADS, pre, jnp.where(lane < LANE_BETA, suf, beta))
    col_ref[...] = cols
    row_ref[0] = cols.T


def _gdn_gates(zs, alog_lane, dtb_lane, b, t):
    spb = t // SUB
    return pl.pallas_call(
        _gdn_gate_kernel,
        grid=(b * spb,),
        in_specs=[pl.BlockSpec((SUB, NZS), lambda i: (i, 0)),
                  pl.BlockSpec((1, NZS), lambda i: (0, 0)),
                  pl.BlockSpec((1, NZS), lambda i: (0, 0))],
        out_specs=[pl.BlockSpec((SUB, NZS), lambda i: (i, 0)),
                   pl.BlockSpec((1, NZS, SUB), lambda i: (i // spb, 0, i % spb))],
        out_shape=[jax.ShapeDtypeStruct((b * t, NZS), F32),
                   jax.ShapeDtypeStruct((b, NZS, t), F32)],
        compiler_params=_cparams(("arbitrary",)),
        name="gdn_gates",
    )(zs, alog_lane, dtb_lane)


GDN_GROUP = 34


def _gdn_kernel(q_ref, k_ref, v_ref, g_ref, cwq_ref, cwk_ref, cwv_ref, col_ref, row_ref, nw_ref, o_ref,
                pad_ref, q32_ref, k32_ref, v32_ref, w_ref, u_ref, qe_ref, sq_ref, sm_ref, att_ref, eg_ref, acc_ref,
                *, n_tok):
    head = pl.program_id(1)
    c_len = GDN_CHUNK
    n_chunks = n_tok // c_len
    n_sub = n_tok // SUB
    grp = GDN_GROUP if n_chunks % GDN_GROUP == 0 else SUB // c_len
    halo = 8

    def pad_row(s):
        return pl.multiple_of(halo + s * SUB + halo * jnp.minimum(s, 1), 8)

    for gap in (0, halo + SUB, 2 * halo + n_tok):
        pad_ref[gap:gap + halo, :] = jnp.zeros((halo, HD), F32)

    for src_ref, cw_ref, dst_ref, norm_scale in ((q_ref, cwq_ref, q32_ref, HD ** -0.5),
                                                 (k_ref, cwk_ref, k32_ref, 1.0),
                                                 (v_ref, cwv_ref, v32_ref, None)):
        def fill(s, _):
            rows = pl.ds(pl.multiple_of(s * SUB, SUB), SUB)
            pad_ref[pl.ds(pad_row(s), SUB), :] = src_ref[0, rows, :].astype(F32)
            return 0

        lax.fori_loop(0, n_sub, fill, 0)
        cw = cw_ref[...]

        def conv(s, _):
            base = pl.multiple_of(pad_row(s) - halo, 8)
            y = None
            for j in range(GDN_CONV):
                tap = pad_ref[pl.ds(base + (halo + j - GDN_CONV // 2), SUB), :] * cw[j:j + 1, :]
                y = tap if y is None else y + tap
            y = _silu(y)
            if norm_scale is not None:
                y = y * (lax.rsqrt(jnp.sum(y * y, axis=-1, keepdims=True) + EPS) * norm_scale)
            dst_ref[pl.ds(pl.multiple_of(s * SUB, SUB), SUB), :] = y
            return 0

        lax.fori_loop(0, n_sub, conv, 0, unroll=True)

    ri = lax.broadcasted_iota(jnp.int32, (grp, c_len, c_len), 1)
    ci = lax.broadcasted_iota(jnp.int32, (grp, c_len, c_len), 2)
    eye = (ri == ci).astype(F32)
    tri = (ri >= ci, ri <= ci)
    strict = (ri > ci, ri < ci)
    merge_masks = []
    m = 1
    while m < c_len:
        merge_masks.append(((ri // (2 * m)) == (ci // (2 * m))) & ((ri // m) != (ci // m)))
        m *= 2

    def bmm(a, b):
        return lax.dot_general(a, b, (((2,), (1,)), ((0,), (0,))), preferred_element_type=F32)

    def bmm_nt(a, b):
        return lax.dot_general(a, b, (((2,), (2,)), ((0,), (0,))), preferred_element_type=F32)

    def bmm_tn(a, b):
        return lax.dot_general(a, b, (((1,), (1,)), ((0,), (0,))), preferred_element_type=F32)

    def local(gi, _):
        r0 = pl.multiple_of(gi * (grp * c_len), grp * c_len)
        rows = pl.ds(r0, grp * c_len)
        q = q32_ref[rows, :].reshape(grp, c_len, HD)
        k = k32_ref[rows, :].reshape(grp, c_len, HD)
        v = v32_ref[rows, :].reshape(grp, c_len, HD)
        kb16 = k.astype(BF16)
        qkk = bmm_nt(jnp.concatenate([q.astype(BF16), kb16], axis=1), kb16)
        qk, kk = qkk[:, :c_len], qkk[:, c_len:]
        cols = col_ref[0, rows, :]
        col_lane = lax.broadcasted_iota(jnp.int32, cols.shape, 1)

        def pick(lane0):
            sel = jnp.where(col_lane == lane0 + head, cols, 0.0)
            return jnp.sum(sel, axis=-1, keepdims=True).reshape(grp, c_len, 1)

        for d in range(2):
            gc = pick(LANE_GC + GDN_HEADS * d)
            beta = pick(LANE_BETA + GDN_HEADS * d)
            gc_row = jnp.stack([row_ref[0, d, 0, pl.ds(gi * grp + g, 1), :] for g in range(grp)], axis=0)
            g_tot = gc[:, c_len - 1:c_len, :] if d == 0 else gc[:, 0:1, :]
            decay = jnp.exp(jnp.where(tri[d], gc - gc_row, -jnp.inf))
            kbeta = k * beta
            a = jnp.where(strict[d], kk * beta * decay, 0.0)
            t = eye - jnp.where(merge_masks[0], a, 0.0)
            for mm in merge_masks[1:]:
                t16 = t.astype(BF16)
                t = t - bmm(bmm(t16, jnp.where(mm, a, 0.0).astype(BF16)).astype(BF16), t16)
            rhs = jnp.concatenate([v * beta, kbeta * jnp.exp(gc)], axis=-1).astype(BF16)
            sol = bmm(t.astype(BF16), rhs)
            u, w16 = sol[:, :, :HD], sol[:, :, HD:].astype(BF16)
            u_ref[d, rows, :] = u.reshape(grp * c_len, HD)
            w_ref[d, rows, :] = w16.reshape(grp * c_len, HD)
            qe_ref[d, rows, :] = (q * jnp.exp(gc)).reshape(grp * c_len, HD).astype(BF16)
            att_ref[d, rows, :] = (qk * decay).reshape(grp * c_len, c_len).astype(BF16)
            kd = (k * jnp.exp(g_tot - gc)).astype(BF16)
            srows = pl.ds(pl.multiple_of(gi * (grp * HD), grp * HD), grp * HD)
            sq_ref[d, srows, :] = bmm_tn(kd, u.astype(BF16)).reshape(grp * HD, HD).astype(BF16)
            sm_ref[d, srows, :] = bmm_tn(kd, w16).reshape(grp * HD, HD).astype(BF16)
            eg_ref[d, pl.ds(pl.multiple_of(gi * grp, grp), grp), :] = jnp.broadcast_to(
                jnp.exp(g_tot).reshape(grp, 1), (grp, HD))
        return 0

    lax.fori_loop(0, n_chunks // grp, local, 0)

    ctx_chunks = SUB // c_len

    def step(d, c, s):
        rows = pl.ds(pl.multiple_of(c * c_len, c_len), c_len)
        srows = pl.ds(pl.multiple_of(c * HD, HD), HD)
        s16 = s.astype(BF16)
        v_new = (u_ref[d, rows, :] - _dot(w_ref[d, rows, :], s16)).astype(BF16)
        acc_ref[d, rows, :] = _dot(qe_ref[d, rows, :], s16) + _dot(att_ref[d, rows, :], v_new)
        return s * eg_ref[d, pl.ds(c, 1), :] + sq_ref[d, srows, :].astype(F32) - _dot(sm_ref[d, srows, :], s16)

    s0 = jnp.zeros((HD, HD), F32)
    scan_unroll = grp

    def scan(i, carry):
        s_f, s_b = carry
        for r in range(scan_unroll):
            j = i * scan_unroll + r
            s_f = step(0, j, s_f)
            s_b = step(1, jnp.where(j < ctx_chunks, ctx_chunks - 1 - j, n_chunks - 1 + ctx_chunks - j), s_b)
        return s_f, s_b

    lax.fori_loop(0, n_chunks // scan_unroll, scan, (s0, s0))

    def finish(s, _):
        rows = pl.ds(pl.multiple_of(s * SUB, SUB), SUB)
        o = acc_ref[0, rows, :] + acc_ref[1, rows, :]
        y = o * lax.rsqrt(jnp.mean(o * o, axis=-1, keepdims=True) + EPS) * nw_ref[...]
        o_ref[0, rows, :] = (_silu(g_ref[0, rows, :].astype(F32)) * y).astype(BF16)
        return 0

    lax.fori_loop(0, n_sub, finish, 0, unroll=True)


def _gdn(z3, conv_w, cols, rows, norm_w):
    b, t, _ = z3.shape
    cb = COL_GDN // HD
    n_chunks = t // GDN_CHUNK

    once = pl.Buffered(1)

    def col(off):
        return pl.BlockSpec((1, t, HD), lambda i, h: (i, 0, cb + off + h), pipeline_mode=once)

    def cw(off):
        return pl.BlockSpec((GDN_CONV, HD), lambda i, h: (0, off + h))

    return pl.pallas_call(
        functools.partial(_gdn_kernel, n_tok=t),
        grid=(b, GDN_HEADS),
        in_specs=[col(0), col(4), col(8), col(12), cw(0), cw(4), cw(8),
                  pl.BlockSpec((1, t, NZS), lambda i, h: (i, 0, 0), pipeline_mode=once),
                  pl.BlockSpec((1, 2, 1, n_chunks, GDN_CHUNK), lambda i, h: (i, 0, h, 0, 0)),
                  pl.BlockSpec((1, HD), lambda i, h: (0, 0))],
        out_specs=pl.BlockSpec((1, t, HD), lambda i, h: (i, 0, h)),
        out_shape=jax.ShapeDtypeStruct((b, t, GDN_HEADS * HD), BF16),
        scratch_shapes=[pltpu.VMEM((t + 24, HD), F32),
                        pltpu.VMEM((t, HD), F32), pltpu.VMEM((t, HD), F32), pltpu.VMEM((t, HD), F32),
                        pltpu.VMEM((2, t, HD), BF16), pltpu.VMEM((2, t, HD), F32),
                        pltpu.VMEM((2, t, HD), BF16),
                        pltpu.VMEM((2, n_chunks * HD, HD), BF16), pltpu.VMEM((2, n_chunks * HD, HD), BF16),
                        pltpu.VMEM((2, t, GDN_CHUNK), BF16),
                        pltpu.VMEM((2, n_chunks, HD), F32),
                        pltpu.VMEM((2, t, HD), F32)],
        compiler_params=_cparams(("arbitrary", "arbitrary")),
        name="gdn",
    )(z3, z3, z3, z3, conv_w, conv_w, conv_w, cols, rows, norm_w.reshape(1, HD))


def _mla_prep_kernel(cq_ref, ckv_ref, zs_ref, qnw_ref, kvnw_ref, wq_ref, wkv_ref, cos_ref, sin_ref,
                     q_ref, k_ref, v_ref):
    def rms(x, w):
        return (x * lax.rsqrt(jnp.mean(x * x, axis=-1, keepdims=True) + EPS) * w).astype(BF16)

    cos, sin = cos_ref[...], sin_ref[...]

    def rope(x):
        return x * cos + _swap_halves(x, 16) * sin

    q = _dot(rms(cq_ref[...].astype(F32), qnw_ref[...]), wq_ref[...])
    kv = _dot(rms(ckv_ref[...].astype(F32), kvnw_ref[...]), wkv_ref[...])
    lane = lax.broadcasted_iota(jnp.int32, (SUB, HD), 1)
    kr = rope(jnp.where(lane < MLA_ROPE, zs_ref[...], 0.0)).astype(BF16)
    for h in range(MLA_HEADS):
        base = h * MLA_QK
        q_ref[0, h, :, 0:HD] = (q[:, base:base + HD] * MLA_Q_SCALE).astype(BF16)
        q_ref[0, h, :, HD:MLA_QK] = (rope(q[:, base + HD:base + MLA_QK]) * MLA_Q_SCALE).astype(BF16)
        k_ref[0, h, :, 0:HD] = kv[:, h * HD:(h + 1) * HD].astype(BF16)
        k_ref[0, h, :, HD:MLA_QK] = kr
        v_ref[0, h] = kv[:, (MLA_HEADS + h) * HD:(MLA_HEADS + h + 1) * HD].astype(BF16)


def _mla_prep(z, zs, qnw, kvnw, wq, wkv, cos, sin, b, t):
    spb = t // SUB

    def tok(width, blk):
        return pl.BlockSpec((SUB, width), lambda i: (i, blk))

    def par(shape):
        return pl.BlockSpec(shape, lambda i: (0, 0))

    def head_out(width):
        return pl.BlockSpec((1, MLA_HEADS, SUB, width), lambda i: (i // spb, 0, i % spb, 0))

    return pl.pallas_call(
        _mla_prep_kernel,
        grid=(b * spb,),
        in_specs=[tok(MLA_Q_RANK, COL_CQ // MLA_Q_RANK), tok(MLA_KV_RANK, COL_CKV // MLA_KV_RANK), tok(NZS, 0),
                  par((1, MLA_Q_RANK)), par((1, MLA_KV_RANK)),
                  par(wq.shape), par(wkv.shape),
                  pl.BlockSpec((SUB, HD), lambda i: (i % spb, 0)),
                  pl.BlockSpec((SUB, HD), lambda i: (i % spb, 0))],
        out_specs=[head_out(MLA_QK), head_out(MLA_QK), head_out(HD)],
        out_shape=[jax.ShapeDtypeStruct((b, MLA_HEADS, t, MLA_QK), BF16),
                   jax.ShapeDtypeStruct((b, MLA_HEADS, t, MLA_QK), BF16),
                   jax.ShapeDtypeStruct((b, MLA_HEADS, t, HD), BF16)],
        compiler_params=_cparams(("arbitrary",)),
        name="mla_prep",
    )(z, z, zs, qnw.reshape(1, -1), kvnw.reshape(1, -1), wq, wkv, cos, sin)


MLA_HEADS_PER_STEP = 8


def _mla_attn_kernel(q_ref, k_ref, v_ref, *gate_and_out_refs):
    g_refs, o_ref = gate_and_out_refs[:-1], gate_and_out_refs[-1]

    def attend(n_keys):
        for hh in range(MLA_HEADS_PER_STEP):
            s = _dot_nt(q_ref[0, hh], k_ref[0, hh, 0:n_keys, :])
            p = jnp.exp2(s - jnp.max(s, axis=-1, keepdims=True))
            l = jnp.sum(p, axis=-1, keepdims=True)
            o = _dot(p.astype(BF16), v_ref[0, hh, 0:n_keys, :]) / l
            gate = _silu(g_refs[hh // 2][0, :, (hh % 2) * HD:(hh % 2 + 1) * HD].astype(F32))
            o_ref[0, :, hh * HD:(hh + 1) * HD] = (gate * o).astype(BF16)

    @pl.when(pl.program_id(2) == 0)
    def _():
        attend(SUB)

    @pl.when(pl.program_id(2) > 0)
    def _():
        attend(k_ref.shape[2])


def _mla_attn(qp, kp, vp, z3):
    b, nh, t, _ = qp.shape
    hs = MLA_HEADS_PER_STEP
    assert hs % 2 == 0 and nh % hs == 0 and COL_MGATE % (2 * HD) == 0
    gb = COL_MGATE // (2 * HD)
    once = pl.Buffered(1)

    def gate_spec(pair):
        return pl.BlockSpec((1, SUB, 2 * HD), lambda i, h, j: (i, j, gb + (hs // 2) * h + pair))

    return pl.pallas_call(
        _mla_attn_kernel,
        grid=(b, nh // hs, t // SUB),
        in_specs=[pl.BlockSpec((1, hs, SUB, MLA_QK), lambda i, h, j: (i, h, j, 0)),
                  pl.BlockSpec((1, hs, t, MLA_QK), lambda i, h, j: (i, h, 0, 0), pipeline_mode=once),
                  pl.BlockSpec((1, hs, t, HD), lambda i, h, j: (i, h, 0, 0), pipeline_mode=once)]
                 + [gate_spec(pair) for pair in range(hs // 2)],
        out_specs=pl.BlockSpec((1, SUB, hs * HD), lambda i, h, j: (i, j, h)),
        out_shape=jax.ShapeDtypeStruct((b, t, nh * HD), BF16),
        compiler_params=_cparams(("arbitrary", "arbitrary", "arbitrary")),
        name="mla_attn",
    )(qp, kp, vp, *([z3] * (hs // 2)))


def _layout_w_in(w_in):
    sizes = (512, 512, 512, 512, 512, 512, 512, 512, 8, 8, MLA_Q_RANK, MLA_KV_RANK, MLA_ROPE, 1024)
    offs = np.concatenate([[0], np.cumsum(sizes)])
    part = lambda n: w_in[:, offs[n]:offs[n + 1]]
    w_main = jnp.concatenate([part(n) for n in (0, 1, 2, 3, 4, 5, 6, 7, 11, 10, 13)], axis=-1)
    w_small = jnp.concatenate([part(12), part(8), part(9)], axis=-1)
    w_small = jnp.pad(w_small, ((0, 0), (0, NZS - w_small.shape[-1])))
    return w_main.astype(BF16), w_small.astype(BF16)


def _layout_w_qb(w_qb):
    w = w_qb.reshape(MLA_Q_RANK, MLA_HEADS, HD + MLA_ROPE)
    w = jnp.pad(w, ((0, 0), (0, 0), (0, MLA_QK - HD - MLA_ROPE)))
    return w.reshape(MLA_Q_RANK, MLA_HEADS * MLA_QK).astype(BF16)


def _layout_w_kvb(w_kvb):
    w = w_kvb.reshape(MLA_KV_RANK, MLA_HEADS, 2, HD)
    return jnp.transpose(w, (0, 2, 1, 3)).reshape(MLA_KV_RANK, 2 * MLA_HEADS * HD).astype(BF16)


def kernel(x, c, ctx, c_ctx, ada_w, ada_b, norm_w, w_in, ret_log_decay, ret_norm_w, gdn_conv_w, gdn_a_log,
           gdn_dt_bias, gdn_norm_w, mla_q_norm_w, mla_w_qb, mla_kv_norm_w, mla_w_kvb, w_out, final_norm_w):
    b, seq, d = x.shape
    n_ctx = ctx.shape[1]
    assert n_ctx == SUB and seq % SUB == 0 and d == D_MODEL
    depth = ada_w.shape[0]
    t = n_ctx + seq
    spb = t // SUB
    n_chunks = t // GDN_CHUNK

    cc = jnp.zeros((16, d), F32).at[:b].set(c).at[b].set(c_ctx)
    mod = _adaln(cc, ada_w, ada_b)
    sel = np.stack([np.full(b, b), np.arange(b)], axis=1).reshape(-1)

    ret_cos, ret_sin = _rope_tables(n_ctx, seq, HD)
    mla_cos, mla_sin = _rope_tables(n_ctx, seq, MLA_ROPE)

    xa = jnp.concatenate([ctx, x], axis=1).reshape(b * t, d)
    for i in range(depth):
        final = i == depth - 1
        shift, scale, gate = mod[i, :, :d], mod[i, :, d:2 * d], mod[i, :, 2 * d:]
        ms = jnp.stack([scale[sel], shift[sel]], axis=1)
        gate_tab = gate[sel][:, None, :]
        w_main, w_small = _layout_w_in(w_in[i])

        z, zs = _inproj(xa, ms, norm_w[i], w_main, w_small, spb)
        z3 = z.reshape(b, t, NZ)

        m_ret = _retention(z3, ret_log_decay[i], ret_norm_w[i], ret_cos, ret_sin)

        on_lanes = lambda p: jnp.zeros((1, NZS), F32).at[0, LANE_GC:LANE_GC + 2 * GDN_HEADS].set(p.reshape(-1))
        cols, rows_t = _gdn_gates(zs, on_lanes(gdn_a_log[i]), on_lanes(gdn_dt_bias[i]), b, t)
        gc_rows = rows_t[:, LANE_GC:LANE_GC + 2 * GDN_HEADS].reshape(b, 2, GDN_HEADS, n_chunks, GDN_CHUNK)
        m_gdn = _gdn(z3, gdn_conv_w[i], cols.reshape(b, t, NZS), gc_rows, gdn_norm_w[i])

        qp, kp, vp = _mla_prep(z, zs, mla_q_norm_w[i], mla_kv_norm_w[i], _layout_w_qb(mla_w_qb[i]),
                               _layout_w_kvb(mla_w_kvb[i]), mla_cos, mla_sin, b, t)
        m_mla = _mla_attn(qp, kp, vp, z3)

        xa = _outproj(xa, m_ret.reshape(b * t, -1), m_gdn.reshape(b * t, -1), m_mla.reshape(b * t, -1),
                      gate_tab, w_out[i].astype(BF16), final_norm_w, spb, final)
    return xa.reshape(b, seq, d)
```

```python
import functools
import math

import jax
import jax.numpy as jnp
import numpy as np
from jax import lax
from jax.experimental import pallas as pl
from jax.experimental.pallas import tpu as pltpu

F32 = jnp.float32
BF16 = jnp.bfloat16

EPS = 1e-6
ROPE_BASE = 10000.0
GRID_W = 64

D_MODEL = 2048
SUB = 256
RET_HEADS = 4
GDN_HEADS = 4
HD = 128
GDN_CONV = 5
GDN_CHUNK = 64
MLA_HEADS = 8
MLA_ROPE = 64
MLA_Q_RANK = 768
MLA_KV_RANK = 512
MLA_QK = 256
MLA_SCALE = (HD + MLA_ROPE) ** -0.5
MLA_Q_SCALE = MLA_SCALE * math.log2(math.e)

COL_RET = 0
COL_GDN = 2048
COL_CKV = 4096
COL_CQ = 4608
COL_MGATE = 5376
NZ = 6400
NZS = 128

VMEM_LIMIT = 56 * 1024 * 1024


def _cparams(sem):
    return pltpu.CompilerParams(dimension_semantics=sem, vmem_limit_bytes=VMEM_LIMIT)


def _silu(x):
    return x * (1.0 / (1.0 + jnp.exp(-x)))


def _dot(a, b):
    return jnp.dot(a, b, preferred_element_type=F32)


def _dot_nt(a, b):
    return lax.dot_general(a, b, (((1,), (1,)), ((), ())), preferred_element_type=F32)


def _dot_tn(a, b):
    return lax.dot_general(a, b, (((0,), (0,)), ((), ())), preferred_element_type=F32)


def _adaln_kernel(c_ref, w_ref, b_ref, o_ref):
    s = _silu(c_ref[...]).astype(BF16)
    o_ref[0] = _dot(s, w_ref[0].astype(BF16)) + b_ref[0]


def _adaln(cc, ada_w, ada_b):
    depth, d, n3 = ada_w.shape
    tn = 768
    return pl.pallas_call(
        _adaln_kernel,
        grid=(depth, n3 // tn),
        in_specs=[pl.BlockSpec((16, d), lambda i, j: (0, 0)),
                  pl.BlockSpec((1, d, tn), lambda i, j: (i, 0, j)),
                  pl.BlockSpec((1, 1, tn), lambda i, j: (i, 0, j))],
        out_specs=pl.BlockSpec((1, 16, tn), lambda i, j: (i, 0, j)),
        out_shape=jax.ShapeDtypeStruct((depth, 16, n3), F32),
        compiler_params=_cparams(("arbitrary", "arbitrary")),
        name="adaln",
    )(cc, ada_w, ada_b.reshape(depth, 1, n3))


def _mod_row(sub_idx, subs_per_batch):
    b = sub_idx // subs_per_batch
    is_lat = (sub_idx % subs_per_batch != 0).astype(jnp.int32)
    return b * 2 + is_lat


def _inproj_kernel(x_ref, ms_ref, nw_ref, w_ref, ws_ref, z_ref, zs_ref, h_ref, *, n_sub, subs_per_batch):
    i = pl.program_id(0)

    @pl.when(pl.program_id(1) == 0)
    def _():
        for u in range(n_sub):
            m = ms_ref[_mod_row(i * n_sub + u, subs_per_batch)]
            x = x_ref[u * SUB:(u + 1) * SUB, :]
            y = x * lax.rsqrt(jnp.mean(x * x, axis=-1, keepdims=True) + EPS) * nw_ref[...]
            h_ref[u * SUB:(u + 1) * SUB, :] = (y * (1.0 + m[0:1]) + m[1:2]).astype(BF16)
        zs_ref[...] = _dot(h_ref[...], ws_ref[...])

    z_ref[...] = _dot(h_ref[...], w_ref[...]).astype(BF16)


def _inproj(xa, ms, norm_w, w_main, w_small, subs_per_batch):
    bt, d = xa.shape
    tm = 1024 if bt % 1024 == 0 else SUB
    tn = 1280
    return pl.pallas_call(
        functools.partial(_inproj_kernel, n_sub=tm // SUB, subs_per_batch=subs_per_batch),
        grid=(bt // tm, NZ // tn),
        in_specs=[pl.BlockSpec((tm, d), lambda i, j: (i, 0)),
                  pl.BlockSpec(ms.shape, lambda i, j: (0, 0, 0)),
                  pl.BlockSpec((1, d), lambda i, j: (0, 0)),
                  pl.BlockSpec((d, tn), lambda i, j: (0, j)),
                  pl.BlockSpec((d, NZS), lambda i, j: (0, 0))],
        out_specs=[pl.BlockSpec((tm, tn), lambda i, j: (i, j)),
                   pl.BlockSpec((tm, NZS), lambda i, j: (i, 0))],
        out_shape=[jax.ShapeDtypeStruct((bt, NZ), BF16),
                   jax.ShapeDtypeStruct((bt, NZS), F32)],
        scratch_shapes=[pltpu.VMEM((tm, d), BF16)],
        compiler_params=_cparams(("arbitrary", "arbitrary")),
        name="inproj",
    )(xa, ms, norm_w.reshape(1, d), w_main, w_small)


def _outproj_kernel(x_ref, r_ref, g_ref, m_ref, gate_ref, w_ref, fw_ref, o_ref, *, n_sub, subs_per_batch, final):
    i = pl.program_id(0)
    acc = (_dot(r_ref[...], w_ref[0:512, :]) + _dot(g_ref[...], w_ref[512:1024, :])
           + _dot(m_ref[...], w_ref[1024:2048, :]))
    for u in range(n_sub):
        row = (i // (subs_per_batch - 1)) * 2 + 1 if final else _mod_row(i * n_sub + u, subs_per_batch)
        gate = gate_ref[row]
        rows = slice(u * SUB, (u + 1) * SUB)
        xn = x_ref[rows, :] + gate * acc[rows, :]
        if final:
            xn = xn * lax.rsqrt(jnp.mean(xn * xn, axis=-1, keepdims=True) + EPS) * fw_ref[...]
        o_ref[rows, :] = xn


def _outproj(xa, mret, mgdn, mmla, gate_tab, w_out, final_w, subs_per_batch, final):
    bt, d = xa.shape
    if final:
        tm, lat = SUB, subs_per_batch - 1
        n_blocks = (bt // SUB // subs_per_batch) * lat
        src = lambda i: ((i // lat) * subs_per_batch + 1 + i % lat, 0)
        aliases = {}
    else:
        tm = 512 if bt % 512 == 0 else SUB
        n_blocks = bt // tm
        src = lambda i: (i, 0)
        aliases = {0: 0}
    return pl.pallas_call(
        functools.partial(_outproj_kernel, n_sub=tm // SUB, subs_per_batch=subs_per_batch, final=final),
        grid=(n_blocks,),
        in_specs=[pl.BlockSpec((tm, d), src),
                  pl.BlockSpec((tm, 512), src),
                  pl.BlockSpec((tm, 512), src),
                  pl.BlockSpec((tm, 1024), src),
                  pl.BlockSpec(gate_tab.shape, lambda i: (0, 0, 0)),
                  pl.BlockSpec((d, d), lambda i: (0, 0)),
                  pl.BlockSpec((1, d), lambda i: (0, 0))],
        out_specs=pl.BlockSpec((tm, d), lambda i: (i, 0)),
        out_shape=jax.ShapeDtypeStruct((n_blocks * tm, d), F32),
        input_output_aliases=aliases,
        compiler_params=_cparams(("arbitrary",)),
        name="outproj",
    )(xa, mret, mgdn, mmla, gate_tab, w_out, final_w.reshape(1, d))


def _swap_halves(x, half):
    lane = lax.broadcasted_iota(jnp.int32, x.shape, x.ndim - 1)
    first = (lane % (2 * half)) < half
    return jnp.where(first, pltpu.roll(x, 128 - half, x.ndim - 1), pltpu.roll(x, half, x.ndim - 1))


def _rope_tables(n_ctx, seq, dim):
    rows = seq // GRID_W
    row = jnp.repeat(jnp.arange(rows), GRID_W).astype(F32)
    col = jnp.tile(jnp.arange(GRID_W), rows).astype(F32)
    m = dim // 2
    inv = ROPE_BASE ** (-jnp.arange(m // 2, dtype=F32) / (m // 2))
    ar, ac = row[:, None] * inv, col[:, None] * inv
    cos = jnp.concatenate([jnp.cos(ar), jnp.cos(ar), jnp.cos(ac), jnp.cos(ac)], axis=-1)
    sin = jnp.concatenate([-jnp.sin(ar), jnp.sin(ar), -jnp.sin(ac), jnp.sin(ac)], axis=-1)
    cos = jnp.pad(cos, ((n_ctx, 0), (0, 128 - dim)), constant_values=1.0)
    sin = jnp.pad(sin, ((n_ctx, 0), (0, 128 - dim)))
    return cos, sin


def _ret_kernel(dec_ref, q_ref, k_ref, v_ref, g_ref, cos_ref, sin_ref, nw_ref, o_ref,
                acc_ref, qd_ref, inc_ref, st_ref, *, n_chunks):
    h = pl.program_id(1)
    gf = dec_ref[0, h]
    gb = dec_ref[1, h]
    c_len = SUB

    ri = lax.broadcasted_iota(jnp.int32, (c_len, c_len), 0)
    ci = lax.broadcasted_iota(jnp.int32, (c_len, c_len), 1)
    dist = (ri - ci).astype(F32)
    mask = (jnp.where(ri >= ci, jnp.exp(gf * jnp.maximum(dist, 0.0)), 0.0)
            + jnp.where(ri <= ci, jnp.exp(gb * jnp.maximum(-dist, 0.0)), 0.0))
    pos = lax.broadcasted_iota(jnp.int32, (c_len, HD), 0).astype(F32)
    q_dec_f = jnp.exp(gf * (pos + 1.0))
    q_dec_b = jnp.exp(gb * (c_len - pos))
    k_dec_f = jnp.exp(gf * (c_len - 1.0 - pos))
    k_dec_b = jnp.exp(gb * pos)
    s_dec_f = jnp.exp(gf * c_len)
    s_dec_b = jnp.exp(gb * c_len)

    def rows_of(c, n=c_len):
        return pl.ds(c * n if isinstance(c, int) else pl.multiple_of(c * n, n), n)

    grp = math.gcd(n_chunks - 1, 16)

    def for_chunks(fn):
        fn(0)

        def body(g, _):
            for r in range(grp):
                fn(1 + g * grp + r)
            return 0

        lax.fori_loop(0, (n_chunks - 1) // grp, body, 0)

    def local(c):
        rows = rows_of(c)
        cos, sin = cos_ref[rows, :], sin_ref[rows, :]
        q = q_ref[0, rows, :].astype(F32)
        k = k_ref[0, rows, :].astype(F32)
        q = q * cos + _swap_halves(q, 32) * sin
        k = (k * cos + _swap_halves(k, 32) * sin) * (HD ** -0.5)
        v = v_ref[0, rows, :]
        p = (_dot_nt(q.astype(BF16), k.astype(BF16)) * mask).astype(BF16)
        acc_ref[rows, :] = _dot(p, v)
        qd_ref[0, rows, :] = (q * q_dec_f).astype(BF16)
        qd_ref[1, rows, :] = (q * q_dec_b).astype(BF16)
        inc_ref[0, rows_of(c, HD), :] = _dot_tn((k * k_dec_f).astype(BF16), v)
        inc_ref[1, rows_of(c, HD), :] = _dot_tn((k * k_dec_b).astype(BF16), v)

    for_chunks(local)

    def scan(i, carry):
        s_f, s_b = carry
        cb = jnp.where(i == 0, 0, n_chunks - i)
        st_ref[0, rows_of(i, HD), :] = s_f.astype(BF16)
        st_ref[1, rows_of(cb, HD), :] = s_b.astype(BF16)
        return (s_f * s_dec_f + inc_ref[0, rows_of(i, HD), :], s_b * s_dec_b + inc_ref[1, rows_of(cb, HD), :])

    lax.fori_loop(0, n_chunks, scan, (jnp.zeros((HD, HD), F32), jnp.zeros((HD, HD), F32)))

    def finish(c):
        rows = rows_of(c)
        o = (acc_ref[rows, :] + _dot(qd_ref[0, rows, :], st_ref[0, rows_of(c, HD), :])
             + _dot(qd_ref[1, rows, :], st_ref[1, rows_of(c, HD), :]))
        y = o * lax.rsqrt(jnp.mean(o * o, axis=-1, keepdims=True) + EPS) * nw_ref[...]
        o_ref[0, rows, :] = (_silu(g_ref[0, rows, :].astype(F32)) * y).astype(BF16)

    for_chunks(finish)


def _retention(z3, log_decay, norm_w, cos, sin):
    b, t, _ = z3.shape
    cb = COL_RET // HD

    def col(off):
        return pl.BlockSpec((1, t, HD), lambda i, h: (i, 0, cb + off + h))

    return pl.pallas_call(
        functools.partial(_ret_kernel, n_chunks=t // SUB),
        grid=(b, RET_HEADS),
        in_specs=[pl.BlockSpec(memory_space=pltpu.SMEM),
                  col(0), col(4), col(8), col(12),
                  pl.BlockSpec((t, HD), lambda i, h: (0, 0)),
                  pl.BlockSpec((t, HD), lambda i, h: (0, 0)),
                  pl.BlockSpec((1, HD), lambda i, h: (0, 0))],
        out_specs=pl.BlockSpec((1, t, HD), lambda i, h: (i, 0, h)),
        out_shape=jax.ShapeDtypeStruct((b, t, RET_HEADS * HD), BF16),
        scratch_shapes=[pltpu.VMEM((t, HD), F32), pltpu.VMEM((2, t, HD), BF16),
                        pltpu.VMEM((2, (t // SUB) * HD, HD), F32), pltpu.VMEM((2, (t // SUB) * HD, HD), BF16)],
        compiler_params=_cparams(("arbitrary", "arbitrary")),
        name="retention",
    )(log_decay, z3, z3, z3, z3, cos, sin, norm_w.reshape(1, HD))


LANE_GC = 64
LANE_BETA = 72


def _gdn_gate_kernel(zs_ref, alog_ref, dtb_ref, col_ref, row_ref):
    x = zs_ref[...]
    z = x + dtb_ref[...]
    softplus = jnp.maximum(z, 0.0) + jnp.log(1.0 + jnp.exp(-jnp.abs(z)))
    g = -jnp.exp(alog_ref[...]) * softplus
    c = GDN_CHUNK
    pos = lax.broadcasted_iota(jnp.int32, (SUB, NZS), 0) % c
    lane = lax.broadcasted_iota(jnp.int32, (SUB, NZS), 1)
    pre = suf = g
    sh = 1
    while sh < c:
        pre = pre + jnp.where(pos >= sh, pltpu.roll(pre, sh, 0), 0.0)
        suf = suf + jnp.where(pos < c - sh, pltpu.roll(suf, SUB - sh, 0), 0.0)
        sh *= 2
    beta = 1.0 / (1.0 + jnp.exp(-x))
    cols = jnp.where(lane < LANE_GC + GDN_HEADS, pre, jnp.where(lane < LANE_BETA, suf, beta))
    col_ref[...] = cols
    row_ref[0] = cols.T


def _gdn_gates(zs, alog_lane, dtb_lane, b, t):
    spb = t // SUB
    return pl.pallas_call(
        _gdn_gate_kernel,
        grid=(b * spb,),
        in_specs=[pl.BlockSpec((SUB, NZS), lambda i: (i, 0)),
                  pl.BlockSpec((1, NZS), lambda i: (0, 0)),
                  pl.BlockSpec((1, NZS), lambda i: (0, 0))],
        out_specs=[pl.BlockSpec((SUB, NZS), lambda i: (i, 0)),
                   pl.BlockSpec((1, NZS, SUB), lambda i: (i // spb, 0, i % spb))],
        out_shape=[jax.ShapeDtypeStruct((b * t, NZS), F32),
                   jax.ShapeDtypeStruct((b, NZS, t), F32)],
        compiler_params=_cparams(("arbitrary",)),
        name="gdn_gates",
    )(zs, alog_lane, dtb_lane)


assert 2 * GDN_CHUNK == HD
GDN_GROUP = 34


def _gdn_kernel(q_ref, k_ref, v_ref, g_ref, cwq_ref, cwk_ref, cwv_ref, col_ref, row_ref, nw_ref, o_ref,
                pad_ref, q32_ref, k32_ref, v32_ref, wq_ref, u_ref, sq_ref, sm_ref, att_ref, eg_ref, acc_ref,
                *, n_tok):
    head = pl.program_id(1)
    c_len = GDN_CHUNK
    n_chunks = n_tok // c_len
    n_sub = n_tok // SUB
    grp = GDN_GROUP if n_chunks % GDN_GROUP == 0 else SUB // c_len
    halo = 8

    def pad_row(s):
        return pl.multiple_of(halo + s * SUB + halo * jnp.minimum(s, 1), 8)

    for gap in (0, halo + SUB, 2 * halo + n_tok):
        pad_ref[gap:gap + halo, :] = jnp.zeros((halo, HD), F32)

    for src_ref, cw_ref, dst_ref, norm_scale in ((q_ref, cwq_ref, q32_ref, HD ** -0.5),
                                                 (k_ref, cwk_ref, k32_ref, 1.0),
                                                 (v_ref, cwv_ref, v32_ref, None)):
        def fill(s, _):
            rows = pl.ds(pl.multiple_of(s * SUB, SUB), SUB)
            pad_ref[pl.ds(pad_row(s), SUB), :] = src_ref[0, rows, :].astype(F32)
            return 0

        lax.fori_loop(0, n_sub, fill, 0)
        cw = cw_ref[...]

        def conv(s, _):
            base = pl.multiple_of(pad_row(s) - halo, 8)
            y = None
            for j in range(GDN_CONV):
                tap = pad_ref[pl.ds(base + (halo + j - GDN_CONV // 2), SUB), :] * cw[j:j + 1, :]
                y = tap if y is None else y + tap
            y = _silu(y)
            if norm_scale is not None:
                y = y * (lax.rsqrt(jnp.sum(y * y, axis=-1, keepdims=True) + EPS) * norm_scale)
            dst_ref[pl.ds(pl.multiple_of(s * SUB, SUB), SUB), :] = y
            return 0

        lax.fori_loop(0, n_sub, conv, 0, unroll=True)

    ri = lax.broadcasted_iota(jnp.int32, (grp, c_len, c_len), 1)
    ci = lax.broadcasted_iota(jnp.int32, (grp, c_len, c_len), 2)
    eye = (ri == ci).astype(F32)
    tri = (ri >= ci, ri <= ci)
    strict = (ri > ci, ri < ci)
    merge_masks = []
    m = 1
    while m < c_len:
        merge_masks.append(((ri // (2 * m)) == (ci // (2 * m))) & ((ri // m) != (ci // m)))
        m *= 2

    def bmm(a, b):
        return lax.dot_general(a, b, (((2,), (1,)), ((0,), (0,))), preferred_element_type=F32)

    def bmm_nt(a, b):
        return lax.dot_general(a, b, (((2,), (2,)), ((0,), (0,))), preferred_element_type=F32)

    def bmm_tn(a, b):
        return lax.dot_general(a, b, (((1,), (1,)), ((0,), (0,))), preferred_element_type=F32)

    def local(gi, _):
        r0 = pl.multiple_of(gi * (grp * c_len), grp * c_len)
        rows = pl.ds(r0, grp * c_len)
        q = q32_ref[rows, :].reshape(grp, c_len, HD)
        k = k32_ref[rows, :].reshape(grp, c_len, HD)
        v = v32_ref[rows, :].reshape(grp, c_len, HD)
        kb16 = k.astype(BF16)
        qkk = bmm_nt(jnp.concatenate([q.astype(BF16), kb16], axis=1), kb16)
        qk, kk = qkk[:, :c_len], qkk[:, c_len:]
        cols = col_ref[0, rows, :]
        col_lane = lax.broadcasted_iota(jnp.int32, cols.shape, 1)

        def pick(lane0):
            sel = jnp.where(col_lane == lane0 + head, cols, 0.0)
            return jnp.sum(sel, axis=-1, keepdims=True).reshape(grp, c_len, 1)

        for d in range(2):
            gc = pick(LANE_GC + GDN_HEADS * d)
            beta = pick(LANE_BETA + GDN_HEADS * d)
            gc_row = jnp.stack([row_ref[0, d, 0, pl.ds(gi * grp + g, 1), :] for g in range(grp)], axis=0)
            g_tot = gc[:, c_len - 1:c_len, :] if d == 0 else gc[:, 0:1, :]
            decay = jnp.exp(jnp.where(tri[d], gc - gc_row, -jnp.inf))
            kbeta = k * beta
            a = jnp.where(strict[d], kk * beta * decay, 0.0)
            t = eye - jnp.where(merge_masks[0], a, 0.0)
            for mm in merge_masks[1:]:
                t16 = t.astype(BF16)
                t = t - bmm(bmm(t16, jnp.where(mm, a, 0.0).astype(BF16)).astype(BF16), t16)
            rhs = jnp.concatenate([v * beta, kbeta * jnp.exp(gc)], axis=-1).astype(BF16)
            sol = bmm(t.astype(BF16), rhs)
            srows = pl.ds(pl.multiple_of(gi * (grp * HD), grp * HD), grp * HD)
            u_ref[d, rows, :] = sol[:, :, :HD].reshape(grp * c_len, HD)
            wq = jnp.concatenate([sol[:, :, HD:], q * jnp.exp(gc)], axis=1).astype(BF16)
            wq_ref[d, srows, :] = wq.reshape(grp * 2 * c_len, HD)
            att_ref[d, rows, :] = (qk * decay).reshape(grp * c_len, c_len).astype(BF16)
            kd = (k * jnp.exp(g_tot - gc)).astype(BF16)
            kd_uw = bmm_tn(kd, sol.astype(BF16)).astype(BF16)
            sq_ref[d, srows, :] = kd_uw[:, :, :HD].reshape(grp * HD, HD)
            sm_ref[d, srows, :] = kd_uw[:, :, HD:].reshape(grp * HD, HD)
            eg_ref[d, pl.ds(pl.multiple_of(gi * grp, grp), grp), :] = jnp.broadcast_to(
                jnp.exp(g_tot).reshape(grp, 1), (grp, HD))
        return 0

    lax.fori_loop(0, n_chunks // grp, local, 0)

    ctx_chunks = SUB // c_len

    def step(d, c, s):
        rows = pl.ds(pl.multiple_of(c * c_len, c_len), c_len)
        srows = pl.ds(pl.multiple_of(c * HD, HD), HD)
        s16 = s.astype(BF16)
        ws_qs = _dot(wq_ref[d, srows, :], s16)
        v_new = (u_ref[d, rows, :] - ws_qs[:c_len]).astype(BF16)
        acc_ref[d, rows, :] = ws_qs[c_len:] + _dot(att_ref[d, rows, :], v_new)
        return s * eg_ref[d, pl.ds(c, 1), :] + sq_ref[d, srows, :].astype(F32) - _dot(sm_ref[d, srows, :], s16)

    s0 = jnp.zeros((HD, HD), F32)
    scan_unroll = grp

    def scan(i, carry):
        s_f, s_b = carry
        for r in range(scan_unroll):
            j = i * scan_unroll + r
            s_f = step(0, j, s_f)
            s_b = step(1, jnp.where(j < ctx_chunks, ctx_chunks - 1 - j, n_chunks - 1 + ctx_chunks - j), s_b)
        return s_f, s_b

    lax.fori_loop(0, n_chunks // scan_unroll, scan, (s0, s0))

    def finish(s, _):
        rows = pl.ds(pl.multiple_of(s * SUB, SUB), SUB)
        o = acc_ref[0, rows, :] + acc_ref[1, rows, :]
        y = o * lax.rsqrt(jnp.mean(o * o, axis=-1, keepdims=True) + EPS) * nw_ref[...]
        o_ref[0, rows, :] = (_silu(g_ref[0, rows, :].astype(F32)) * y).astype(BF16)
        return 0

    lax.fori_loop(0, n_sub, finish, 0, unroll=True)


def _gdn(z3, conv_w, cols, rows, norm_w):
    b, t, _ = z3.shape
    cb = COL_GDN // HD
    n_chunks = t // GDN_CHUNK

    once = pl.Buffered(1)

    def col(off):
        return pl.BlockSpec((1, t, HD), lambda i, h: (i, 0, cb + off + h), pipeline_mode=once)

    def cw(off):
        return pl.BlockSpec((GDN_CONV, HD), lambda i, h: (0, off + h))

    return pl.pallas_call(
        functools.partial(_gdn_kernel, n_tok=t),
        grid=(b, GDN_HEADS),
        in_specs=[col(0), col(4), col(8), col(12), cw(0), cw(4), cw(8),
                  pl.BlockSpec((1, t, NZS), lambda i, h: (i, 0, 0), pipeline_mode=once),
                  pl.BlockSpec((1, 2, 1, n_chunks, GDN_CHUNK), lambda i, h: (i, 0, h, 0, 0)),
                  pl.BlockSpec((1, HD), lambda i, h: (0, 0))],
        out_specs=pl.BlockSpec((1, t, HD), lambda i, h: (i, 0, h)),
        out_shape=jax.ShapeDtypeStruct((b, t, GDN_HEADS * HD), BF16),
        scratch_shapes=[pltpu.VMEM((t + 24, HD), F32),
                        pltpu.VMEM((t, HD), F32), pltpu.VMEM((t, HD), F32), pltpu.VMEM((t, HD), F32),
                        pltpu.VMEM((2, 2 * t, HD), BF16), pltpu.VMEM((2, t, HD), F32),
                        pltpu.VMEM((2, n_chunks * HD, HD), BF16), pltpu.VMEM((2, n_chunks * HD, HD), BF16),
                        pltpu.VMEM((2, t, GDN_CHUNK), BF16),
                        pltpu.VMEM((2, n_chunks, HD), F32),
                        pltpu.VMEM((2, t, HD), F32)],
        compiler_params=_cparams(("arbitrary", "arbitrary")),
        name="gdn",
    )(z3, z3, z3, z3, conv_w, conv_w, conv_w, cols, rows, norm_w.reshape(1, HD))


def _mla_prep_kernel(cq_ref, ckv_ref, zs_ref, qnw_ref, kvnw_ref, wq_ref, wkv_ref, cos_ref, sin_ref,
                     q_ref, k_ref, v_ref):
    def rms(x, w):
        return (x * lax.rsqrt(jnp.mean(x * x, axis=-1, keepdims=True) + EPS) * w).astype(BF16)

    cos, sin = cos_ref[...], sin_ref[...]

    def rope(x):
        return x * cos + _swap_halves(x, 16) * sin

    q = _dot(rms(cq_ref[...].astype(F32), qnw_ref[...]), wq_ref[...])
    kv = _dot(rms(ckv_ref[...].astype(F32), kvnw_ref[...]), wkv_ref[...])
    lane = lax.broadcasted_iota(jnp.int32, (SUB, HD), 1)
    kr = rope(jnp.where(lane < MLA_ROPE, zs_ref[...], 0.0)).astype(BF16)
    for h in range(MLA_HEADS):
        base = h * MLA_QK
        q_ref[0, h, :, 0:HD] = (q[:, base:base + HD] * MLA_Q_SCALE).astype(BF16)
        q_ref[0, h, :, HD:MLA_QK] = (rope(q[:, base + HD:base + MLA_QK]) * MLA_Q_SCALE).astype(BF16)
        k_ref[0, h, :, 0:HD] = kv[:, h * HD:(h + 1) * HD].astype(BF16)
        k_ref[0, h, :, HD:MLA_QK] = kr
        v_ref[0, h] = kv[:, (MLA_HEADS + h) * HD:(MLA_HEADS + h + 1) * HD].astype(BF16)


def _mla_prep(z, zs, qnw, kvnw, wq, wkv, cos, sin, b, t):
    spb = t // SUB

    def tok(width, blk):
        return pl.BlockSpec((SUB, width), lambda i: (i, blk))

    def par(shape):
        return pl.BlockSpec(shape, lambda i: (0, 0))

    def head_out(width):
        return pl.BlockSpec((1, MLA_HEADS, SUB, width), lambda i: (i // spb, 0, i % spb, 0))

    return pl.pallas_call(
        _mla_prep_kernel,
        grid=(b * spb,),
        in_specs=[tok(MLA_Q_RANK, COL_CQ // MLA_Q_RANK), tok(MLA_KV_RANK, COL_CKV // MLA_KV_RANK), tok(NZS, 0),
                  par((1, MLA_Q_RANK)), par((1, MLA_KV_RANK)),
                  par(wq.shape), par(wkv.shape),
                  pl.BlockSpec((SUB, HD), lambda i: (i % spb, 0)),
                  pl.BlockSpec((SUB, HD), lambda i: (i % spb, 0))],
        out_specs=[head_out(MLA_QK), head_out(MLA_QK), head_out(HD)],
        out_shape=[jax.ShapeDtypeStruct((b, MLA_HEADS, t, MLA_QK), BF16),
                   jax.ShapeDtypeStruct((b, MLA_HEADS, t, MLA_QK), BF16),
                   jax.ShapeDtypeStruct((b, MLA_HEADS, t, HD), BF16)],
        compiler_params=_cparams(("arbitrary",)),
        name="mla_prep",
    )(z, z, zs, qnw.reshape(1, -1), kvnw.reshape(1, -1), wq, wkv, cos, sin)


MLA_HEADS_PER_STEP = 8


def _mla_attn_kernel(q_ref, k_ref, v_ref, *gate_and_out_refs):
    g_refs, o_ref = gate_and_out_refs[:-1], gate_and_out_refs[-1]

    def attend(n_keys):
        for hh in range(MLA_HEADS_PER_STEP):
            s = _dot_nt(q_ref[0, hh], k_ref[0, hh, 0:n_keys, :])
            p = jnp.exp2(s - jnp.max(s, axis=-1, keepdims=True))
            l = jnp.sum(p, axis=-1, keepdims=True)
            o = _dot(p.astype(BF16), v_ref[0, hh, 0:n_keys, :]) / l
            gate = _silu(g_refs[hh // 2][0, :, (hh % 2) * HD:(hh % 2 + 1) * HD].astype(F32))
            o_ref[0, :, hh * HD:(hh + 1) * HD] = (gate * o).astype(BF16)

    @pl.when(pl.program_id(2) == 0)
    def _():
        attend(SUB)

    @pl.when(pl.program_id(2) > 0)
    def _():
        attend(k_ref.shape[2])


def _mla_attn(qp, kp, vp, z3):
    b, nh, t, _ = qp.shape
    hs = MLA_HEADS_PER_STEP
    assert hs % 2 == 0 and nh % hs == 0 and COL_MGATE % (2 * HD) == 0
    gb = COL_MGATE // (2 * HD)
    once = pl.Buffered(1)

    def gate_spec(pair):
        return pl.BlockSpec((1, SUB, 2 * HD), lambda i, h, j: (i, j, gb + (hs // 2) * h + pair))

    return pl.pallas_call(
        _mla_attn_kernel,
        grid=(b, nh // hs, t // SUB),
        in_specs=[pl.BlockSpec((1, hs, SUB, MLA_QK), lambda i, h, j: (i, h, j, 0)),
                  pl.BlockSpec((1, hs, t, MLA_QK), lambda i, h, j: (i, h, 0, 0), pipeline_mode=once),
                  pl.BlockSpec((1, hs, t, HD), lambda i, h, j: (i, h, 0, 0), pipeline_mode=once)]
                 + [gate_spec(pair) for pair in range(hs // 2)],
        out_specs=pl.BlockSpec((1, SUB, hs * HD), lambda i, h, j: (i, j, h)),
        out_shape=jax.ShapeDtypeStruct((b, t, nh * HD), BF16),
        compiler_params=_cparams(("arbitrary", "arbitrary", "arbitrary")),
        name="mla_attn",
    )(qp, kp, vp, *([z3] * (hs // 2)))


def _layout_w_in(w_in):
    sizes = (512, 512, 512, 512, 512, 512, 512, 512, 8, 8, MLA_Q_RANK, MLA_KV_RANK, MLA_ROPE, 1024)
    offs = np.concatenate([[0], np.cumsum(sizes)])
    part = lambda n: w_in[:, offs[n]:offs[n + 1]]
    w_main = jnp.concatenate([part(n) for n in (0, 1, 2, 3, 4, 5, 6, 7, 11, 10, 13)], axis=-1)
    w_small = jnp.concatenate([part(12), part(8), part(9)], axis=-1)
    w_small = jnp.pad(w_small, ((0, 0), (0, NZS - w_small.shape[-1])))
    return w_main.astype(BF16), w_small.astype(BF16)


def _layout_w_qb(w_qb):
    w = w_qb.reshape(MLA_Q_RANK, MLA_HEADS, HD + MLA_ROPE)
    w = jnp.pad(w, ((0, 0), (0, 0), (0, MLA_QK - HD - MLA_ROPE)))
    return w.reshape(MLA_Q_RANK, MLA_HEADS * MLA_QK).astype(BF16)


def _layout_w_kvb(w_kvb):
    w = w_kvb.reshape(MLA_KV_RANK, MLA_HEADS, 2, HD)
    return jnp.transpose(w, (0, 2, 1, 3)).reshape(MLA_KV_RANK, 2 * MLA_HEADS * HD).astype(BF16)


def kernel(x, c, ctx, c_ctx, ada_w, ada_b, norm_w, w_in, ret_log_decay, ret_norm_w, gdn_conv_w, gdn_a_log,
           gdn_dt_bias, gdn_norm_w, mla_q_norm_w, mla_w_qb, mla_kv_norm_w, mla_w_kvb, w_out, final_norm_w):
    b, seq, d = x.shape
    n_ctx = ctx.shape[1]
    assert n_ctx == SUB and seq % SUB == 0 and d == D_MODEL
    depth = ada_w.shape[0]
    t = n_ctx + seq
    spb = t // SUB
    n_chunks = t // GDN_CHUNK

    cc = jnp.zeros((16, d), F32).at[:b].set(c).at[b].set(c_ctx)
    mod = _adaln(cc, ada_w, ada_b)
    sel = np.stack([np.full(b, b), np.arange(b)], axis=1).reshape(-1)

    ret_cos, ret_sin = _rope_tables(n_ctx, seq, HD)
    mla_cos, mla_sin = _rope_tables(n_ctx, seq, MLA_ROPE)

    xa = jnp.concatenate([ctx, x], axis=1).reshape(b * t, d)
    for i in range(depth):
        final = i == depth - 1
        shift, scale, gate = mod[i, :, :d], mod[i, :, d:2 * d], mod[i, :, 2 * d:]
        ms = jnp.stack([scale[sel], shift[sel]], axis=1)
        gate_tab = gate[sel][:, None, :]
        w_main, w_small = _layout_w_in(w_in[i])

        z, zs = _inproj(xa, ms, norm_w[i], w_main, w_small, spb)
        z3 = z.reshape(b, t, NZ)

        m_ret = _retention(z3, ret_log_decay[i], ret_norm_w[i], ret_cos, ret_sin)

        on_lanes = lambda p: jnp.zeros((1, NZS), F32).at[0, LANE_GC:LANE_GC + 2 * GDN_HEADS].set(p.reshape(-1))
        cols, rows_t = _gdn_gates(zs, on_lanes(gdn_a_log[i]), on_lanes(gdn_dt_bias[i]), b, t)
        gc_rows = rows_t[:, LANE_GC:LANE_GC + 2 * GDN_HEADS].reshape(b, 2, GDN_HEADS, n_chunks, GDN_CHUNK)
        m_gdn = _gdn(z3, gdn_conv_w[i], cols.reshape(b, t, NZS), gc_rows, gdn_norm_w[i])

        qp, kp, vp = _mla_prep(z, zs, mla_q_norm_w[i], mla_kv_norm_w[i], _layout_w_qb(mla_w_qb[i]),
                               _layout_w_kvb(mla_w_kvb[i]), mla_cos, mla_sin, b, t)
        m_mla = _mla_attn(qp, kp, vp, z3)

        xa = _outproj(xa, m_ret.reshape(b * t, -1), m_gdn.reshape(b * t, -1), m_mla.reshape(b * t, -1),
                      gate_tab, w_out[i].astype(BF16), final_norm_w, spb, final)
    return xa.reshape(b, seq, d)
```

```python
import functools
import math

import jax
import jax.numpy as jnp
import numpy as np
from jax import lax
from jax.experimental import pallas as pl
from jax.experimental.pallas import tpu as pltpu

F32 = jnp.float32
BF16 = jnp.bfloat16

EPS = 1e-6
ROPE_BASE = 10000.0
GRID_W = 64

D_MODEL = 2048
SUB = 256
RET_HEADS = 4
GDN_HEADS = 4
HD = 128
GDN_CONV = 5
GDN_CHUNK = 64
MLA_HEADS = 8
MLA_ROPE = 64
MLA_Q_RANK = 768
MLA_KV_RANK = 512
MLA_QK = 256
MLA_SCALE = (HD + MLA_ROPE) ** -0.5
MLA_Q_SCALE = MLA_SCALE * math.log2(math.e)

COL_RET = 0
COL_GDN = 2048
COL_CKV = 4096
COL_CQ = 4608
COL_MGATE = 5376
NZ = 6400
NZS = 128

VMEM_LIMIT = 56 * 1024 * 1024


def _cparams(sem):
    return pltpu.CompilerParams(dimension_semantics=sem, vmem_limit_bytes=VMEM_LIMIT)


def _silu(x):
    return x * (1.0 / (1.0 + jnp.exp(-x)))


def _dot(a, b):
    return jnp.dot(a, b, preferred_element_type=F32)


def _dot_nt(a, b):
    return lax.dot_general(a, b, (((1,), (1,)), ((), ())), preferred_element_type=F32)


def _dot_tn(a, b):
    return lax.dot_general(a, b, (((0,), (0,)), ((), ())), preferred_element_type=F32)


def _adaln_kernel(c_ref, w_ref, b_ref, o_ref):
    s = _silu(c_ref[...]).astype(BF16)
    o_ref[0] = _dot(s, w_ref[0].astype(BF16)) + b_ref[0]


def _adaln(cc, ada_w, ada_b):
    depth, d, n3 = ada_w.shape
    tn = 768
    return pl.pallas_call(
        _adaln_kernel,
        grid=(depth, n3 // tn),
        in_specs=[pl.BlockSpec((16, d), lambda i, j: (0, 0)),
                  pl.BlockSpec((1, d, tn), lambda i, j: (i, 0, j)),
                  pl.BlockSpec((1, 1, tn), lambda i, j: (i, 0, j))],
        out_specs=pl.BlockSpec((1, 16, tn), lambda i, j: (i, 0, j)),
        out_shape=jax.ShapeDtypeStruct((depth, 16, n3), F32),
        compiler_params=_cparams(("arbitrary", "arbitrary")),
        name="adaln",
    )(cc, ada_w, ada_b.reshape(depth, 1, n3))


def _mod_row(sub_idx, subs_per_batch):
    b = sub_idx // subs_per_batch
    is_lat = (sub_idx % subs_per_batch != 0).astype(jnp.int32)
    return b * 2 + is_lat


def _inproj_kernel(x_ref, ms_ref, nw_ref, w_ref, ws_ref, z_ref, zs_ref, h_ref, *, n_sub, subs_per_batch):
    i = pl.program_id(0)

    @pl.when(pl.program_id(1) == 0)
    def _():
        for u in range(n_sub):
            m = ms_ref[_mod_row(i * n_sub + u, subs_per_batch)]
            x = x_ref[u * SUB:(u + 1) * SUB, :]
            y = x * lax.rsqrt(jnp.mean(x * x, axis=-1, keepdims=True) + EPS) * nw_ref[...]
            h_ref[u * SUB:(u + 1) * SUB, :] = (y * (1.0 + m[0:1]) + m[1:2]).astype(BF16)
        zs_ref[...] = _dot(h_ref[...], ws_ref[...])

    z_ref[...] = _dot(h_ref[...], w_ref[...]).astype(BF16)


def _inproj(xa, ms, norm_w, w_main, w_small, subs_per_batch):
    bt, d = xa.shape
    tm = 1024 if bt % 1024 == 0 else SUB
    tn = 1280
    return pl.pallas_call(
        functools.partial(_inproj_kernel, n_sub=tm // SUB, subs_per_batch=subs_per_batch),
        grid=(bt // tm, NZ // tn),
        in_specs=[pl.BlockSpec((tm, d), lambda i, j: (i, 0)),
                  pl.BlockSpec(ms.shape, lambda i, j: (0, 0, 0)),
                  pl.BlockSpec((1, d), lambda i, j: (0, 0)),
                  pl.BlockSpec((d, tn), lambda i, j: (0, j)),
                  pl.BlockSpec((d, NZS), lambda i, j: (0, 0))],
        out_specs=[pl.BlockSpec((tm, tn), lambda i, j: (i, j)),
                   pl.BlockSpec((tm, NZS), lambda i, j: (i, 0))],
        out_shape=[jax.ShapeDtypeStruct((bt, NZ), BF16),
                   jax.ShapeDtypeStruct((bt, NZS), F32)],
        scratch_shapes=[pltpu.VMEM((tm, d), BF16)],
        compiler_params=_cparams(("arbitrary", "arbitrary")),
        name="inproj",
    )(xa, ms, norm_w.reshape(1, d), w_main, w_small)


def _outproj_kernel(x_ref, r_ref, g_ref, m_ref, gate_ref, w_ref, fw_ref, o_ref, *, n_sub, subs_per_batch, final):
    i = pl.program_id(0)
    acc = (_dot(r_ref[...], w_ref[0:512, :]) + _dot(g_ref[...], w_ref[512:1024, :])
           + _dot(m_ref[...], w_ref[1024:2048, :]))
    for u in range(n_sub):
        row = (i // (subs_per_batch - 1)) * 2 + 1 if final else _mod_row(i * n_sub + u, subs_per_batch)
        gate = gate_ref[row]
        rows = slice(u * SUB, (u + 1) * SUB)
        xn = x_ref[rows, :] + gate * acc[rows, :]
        if final:
            xn = xn * lax.rsqrt(jnp.mean(xn * xn, axis=-1, keepdims=True) + EPS) * fw_ref[...]
        o_ref[rows, :] = xn


def _outproj(xa, mret, mgdn, mmla, gate_tab, w_out, final_w, subs_per_batch, final):
    bt, d = xa.shape
    if final:
        tm, lat = SUB, subs_per_batch - 1
        n_blocks = (bt // SUB // subs_per_batch) * lat
        src = lambda i: ((i // lat) * subs_per_batch + 1 + i % lat, 0)
        aliases = {}
    else:
        tm = 512 if bt % 512 == 0 else SUB
        n_blocks = bt // tm
        src = lambda i: (i, 0)
        aliases = {0: 0}
    return pl.pallas_call(
        functools.partial(_outproj_kernel, n_sub=tm // SUB, subs_per_batch=subs_per_batch, final=final),
        grid=(n_blocks,),
        in_specs=[pl.BlockSpec((tm, d), src),
                  pl.BlockSpec((tm, 512), src),
                  pl.BlockSpec((tm, 512), src),
                  pl.BlockSpec((tm, 1024), src),
                  pl.BlockSpec(gate_tab.shape, lambda i: (0, 0, 0)),
                  pl.BlockSpec((d, d), lambda i: (0, 0)),
                  pl.BlockSpec((1, d), lambda i: (0, 0))],
        out_specs=pl.BlockSpec((tm, d), lambda i: (i, 0)),
        out_shape=jax.ShapeDtypeStruct((n_blocks * tm, d), F32),
        input_output_aliases=aliases,
        compiler_params=_cparams(("arbitrary",)),
        name="outproj",
    )(xa, mret, mgdn, mmla, gate_tab, w_out, final_w.reshape(1, d))


def _swap_halves(x, half):
    lane = lax.broadcasted_iota(jnp.int32, x.shape, x.ndim - 1)
    first = (lane % (2 * half)) < half
    return jnp.where(first, pltpu.roll(x, 128 - half, x.ndim - 1), pltpu.roll(x, half, x.ndim - 1))


def _rope_tables(n_ctx, seq, dim):
    rows = seq // GRID_W
    row = jnp.repeat(jnp.arange(rows), GRID_W).astype(F32)
    col = jnp.tile(jnp.arange(GRID_W), rows).astype(F32)
    m = dim // 2
    inv = ROPE_BASE ** (-jnp.arange(m // 2, dtype=F32) / (m // 2))
    ar, ac = row[:, None] * inv, col[:, None] * inv
    cos = jnp.concatenate([jnp.cos(ar), jnp.cos(ar), jnp.cos(ac), jnp.cos(ac)], axis=-1)
    sin = jnp.concatenate([-jnp.sin(ar), jnp.sin(ar), -jnp.sin(ac), jnp.sin(ac)], axis=-1)
    cos = jnp.pad(cos, ((n_ctx, 0), (0, 128 - dim)), constant_values=1.0)
    sin = jnp.pad(sin, ((n_ctx, 0), (0, 128 - dim)))
    return cos, sin


def _ret_kernel(dec_ref, q_ref, k_ref, v_ref, g_ref, cos_ref, sin_ref, nw_ref, o_ref,
                acc_ref, qd_ref, inc_ref, st_ref, *, n_chunks):
    h = pl.program_id(1)
    gf = dec_ref[0, h]
    gb = dec_ref[1, h]
    c_len = SUB

    ri = lax.broadcasted_iota(jnp.int32, (c_len, c_len), 0)
    ci = lax.broadcasted_iota(jnp.int32, (c_len, c_len), 1)
    dist = (ri - ci).astype(F32)
    mask = (jnp.where(ri >= ci, jnp.exp(gf * jnp.maximum(dist, 0.0)), 0.0)
            + jnp.where(ri <= ci, jnp.exp(gb * jnp.maximum(-dist, 0.0)), 0.0))
    pos = lax.broadcasted_iota(jnp.int32, (c_len, HD), 0).astype(F32)
    q_dec_f = jnp.exp(gf * (pos + 1.0))
    q_dec_b = jnp.exp(gb * (c_len - pos))
    k_dec_f = jnp.exp(gf * (c_len - 1.0 - pos))
    k_dec_b = jnp.exp(gb * pos)
    s_dec_f = jnp.exp(gf * c_len)
    s_dec_b = jnp.exp(gb * c_len)

    def rows_of(c, n=c_len):
        return pl.ds(c * n if isinstance(c, int) else pl.multiple_of(c * n, n), n)

    grp = math.gcd(n_chunks - 1, 16)

    def for_chunks(fn):
        fn(0)

        def body(g, _):
            for r in range(grp):
                fn(1 + g * grp + r)
            return 0

        lax.fori_loop(0, (n_chunks - 1) // grp, body, 0)

    def local(c):
        rows = rows_of(c)
        cos, sin = cos_ref[rows, :], sin_ref[rows, :]
        q = q_ref[0, rows, :].astype(F32)
        k = k_ref[0, rows, :].astype(F32)
        q = q * cos + _swap_halves(q, 32) * sin
        k = (k * cos + _swap_halves(k, 32) * sin) * (HD ** -0.5)
        v = v_ref[0, rows, :]
        p = (_dot_nt(q.astype(BF16), k.astype(BF16)) * mask).astype(BF16)
        acc_ref[rows, :] = _dot(p, v)
        qd_ref[0, rows, :] = (q * q_dec_f).astype(BF16)
        qd_ref[1, rows, :] = (q * q_dec_b).astype(BF16)
        inc_ref[0, rows_of(c, HD), :] = _dot_tn((k * k_dec_f).astype(BF16), v)
        inc_ref[1, rows_of(c, HD), :] = _dot_tn((k * k_dec_b).astype(BF16), v)

    for_chunks(local)

    def scan(i, carry):
        s_f, s_b = carry
        cb = jnp.where(i == 0, 0, n_chunks - i)
        st_ref[0, rows_of(i, HD), :] = s_f.astype(BF16)
        st_ref[1, rows_of(cb, HD), :] = s_b.astype(BF16)
        return (s_f * s_dec_f + inc_ref[0, rows_of(i, HD), :], s_b * s_dec_b + inc_ref[1, rows_of(cb, HD), :])

    lax.fori_loop(0, n_chunks, scan, (jnp.zeros((HD, HD), F32), jnp.zeros((HD, HD), F32)))

    def finish(c):
        rows = rows_of(c)
        o = (acc_ref[rows, :] + _dot(qd_ref[0, rows, :], st_ref[0, rows_of(c, HD), :])
             + _dot(qd_ref[1, rows, :], st_ref[1, rows_of(c, HD), :]))
        y = o * lax.rsqrt(jnp.mean(o * o, axis=-1, keepdims=True) + EPS) * nw_ref[...]
        o_ref[0, rows, :] = (_silu(g_ref[0, rows, :].astype(F32)) * y).astype(BF16)

    for_chunks(finish)


def _retention(z3, log_decay, norm_w, cos, sin):
    b, t, _ = z3.shape
    cb = COL_RET // HD

    def col(off):
        return pl.BlockSpec((1, t, HD), lambda i, h: (i, 0, cb + off + h))

    return pl.pallas_call(
        functools.partial(_ret_kernel, n_chunks=t // SUB),
        grid=(b, RET_HEADS),
        in_specs=[pl.BlockSpec(memory_space=pltpu.SMEM),
                  col(0), col(4), col(8), col(12),
                  pl.BlockSpec((t, HD), lambda i, h: (0, 0)),
                  pl.BlockSpec((t, HD), lambda i, h: (0, 0)),
                  pl.BlockSpec((1, HD), lambda i, h: (0, 0))],
        out_specs=pl.BlockSpec((1, t, HD), lambda i, h: (i, 0, h)),
        out_shape=jax.ShapeDtypeStruct((b, t, RET_HEADS * HD), BF16),
        scratch_shapes=[pltpu.VMEM((t, HD), F32), pltpu.VMEM((2, t, HD), BF16),
                        pltpu.VMEM((2, (t // SUB) * HD, HD), F32), pltpu.VMEM((2, (t // SUB) * HD, HD), BF16)],
        compiler_params=_cparams(("arbitrary", "arbitrary")),
        name="retention",
    )(log_decay, z3, z3, z3, z3, cos, sin, norm_w.reshape(1, HD))


LANE_GC = 64
LANE_BETA = 72


def _gdn_gate_kernel(zs_ref, alog_ref, dtb_ref, col_ref, row_ref):
    c = GDN_CHUNK
    pos = lax.broadcasted_iota(jnp.int32, (SUB, NZS), 0) % c
    lane = lax.broadcasted_iota(jnp.int32, (SUB, NZS), 1)
    for s in range(zs_ref.shape[0] // SUB):
        rows = slice(s * SUB, (s + 1) * SUB)
        x = zs_ref[rows, :]
        z = x + dtb_ref[...]
        softplus = jnp.maximum(z, 0.0) + jnp.log(1.0 + jnp.exp(-jnp.abs(z)))
        g = -jnp.exp(alog_ref[...]) * softplus
        pre = suf = g
        sh = 1
        while sh < c:
            pre = pre + jnp.where(pos >= sh, pltpu.roll(pre, sh, 0), 0.0)
            suf = suf + jnp.where(pos < c - sh, pltpu.roll(suf, SUB - sh, 0), 0.0)
            sh *= 2
        beta = 1.0 / (1.0 + jnp.exp(-x))
        cols = jnp.where(lane < LANE_GC + GDN_HEADS, pre, jnp.where(lane < LANE_BETA, suf, beta))
        col_ref[rows, :] = cols
        row_ref[0, :, rows] = cols.T


def _gdn_gates(zs, alog_lane, dtb_lane, b, t):
    return pl.pallas_call(
        _gdn_gate_kernel,
        grid=(b,),
        in_specs=[pl.BlockSpec((t, NZS), lambda i: (i, 0)),
                  pl.BlockSpec((1, NZS), lambda i: (0, 0)),
                  pl.BlockSpec((1, NZS), lambda i: (0, 0))],
        out_specs=[pl.BlockSpec((t, NZS), lambda i: (i, 0)),
                   pl.BlockSpec((1, NZS, t), lambda i: (i, 0, 0))],
        out_shape=[jax.ShapeDtypeStruct((b * t, NZS), F32),
                   jax.ShapeDtypeStruct((b, NZS, t), F32)],
        compiler_params=_cparams(("arbitrary",)),
        name="gdn_gates",
    )(zs, alog_lane, dtb_lane)


assert 2 * GDN_CHUNK == HD
GDN_GROUP = 34


def _gdn_kernel(q_ref, k_ref, v_ref, g_ref, cwq_ref, cwk_ref, cwv_ref, col_ref, row_ref, nw_ref, o_ref,
                pad_ref, q32_ref, k32_ref, v32_ref, wq_ref, u_ref, sq_ref, sm_ref, att_ref, eg_ref, acc_ref,
                *, n_tok):
    head = pl.program_id(1)
    c_len = GDN_CHUNK
    n_chunks = n_tok // c_len
    n_sub = n_tok // SUB
    grp = GDN_GROUP if n_chunks % GDN_GROUP == 0 else SUB // c_len
    halo = 8

    def pad_row(s):
        return pl.multiple_of(halo + s * SUB + halo * jnp.minimum(s, 1), 8)

    for gap in (0, halo + SUB, 2 * halo + n_tok):
        pad_ref[gap:gap + halo, :] = jnp.zeros((halo, HD), F32)

    for src_ref, cw_ref, dst_ref, norm_scale in ((q_ref, cwq_ref, q32_ref, HD ** -0.5),
                                                 (k_ref, cwk_ref, k32_ref, 1.0),
                                                 (v_ref, cwv_ref, v32_ref, None)):
        def fill(s, _):
            rows = pl.ds(pl.multiple_of(s * SUB, SUB), SUB)
            pad_ref[pl.ds(pad_row(s), SUB), :] = src_ref[0, rows, :].astype(F32)
            return 0

        lax.fori_loop(0, n_sub, fill, 0)
        cw = cw_ref[...]

        def conv(s, _):
            base = pl.multiple_of(pad_row(s) - halo, 8)
            y = None
            for j in range(GDN_CONV):
                tap = pad_ref[pl.ds(base + (halo + j - GDN_CONV // 2), SUB), :] * cw[j:j + 1, :]
                y = tap if y is None else y + tap
            y = _silu(y)
            if norm_scale is not None:
                y = y * (lax.rsqrt(jnp.sum(y * y, axis=-1, keepdims=True) + EPS) * norm_scale)
            dst_ref[pl.ds(pl.multiple_of(s * SUB, SUB), SUB), :] = y
            return 0

        lax.fori_loop(0, n_sub, conv, 0, unroll=True)

    ri = lax.broadcasted_iota(jnp.int32, (grp, c_len, c_len), 1)
    ci = lax.broadcasted_iota(jnp.int32, (grp, c_len, c_len), 2)
    eye = (ri == ci).astype(F32)
    tri = (ri >= ci, ri <= ci)
    strict = (ri > ci, ri < ci)
    merge_masks = []
    m = 1
    while m < c_len:
        merge_masks.append(((ri // (2 * m)) == (ci // (2 * m))) & ((ri // m) != (ci // m)))
        m *= 2

    def bmm(a, b):
        return lax.dot_general(a, b, (((2,), (1,)), ((0,), (0,))), preferred_element_type=F32)

    def bmm_nt(a, b):
        return lax.dot_general(a, b, (((2,), (2,)), ((0,), (0,))), preferred_element_type=F32)

    def bmm_tn(a, b):
        return lax.dot_general(a, b, (((1,), (1,)), ((0,), (0,))), preferred_element_type=F32)

    def local(gi, _):
        r0 = pl.multiple_of(gi * (grp * c_len), grp * c_len)
        rows = pl.ds(r0, grp * c_len)
        q = q32_ref[rows, :].reshape(grp, c_len, HD)
        k = k32_ref[rows, :].reshape(grp, c_len, HD)
        v = v32_ref[rows, :].reshape(grp, c_len, HD)
        kb16 = k.astype(BF16)
        qkk = bmm_nt(jnp.concatenate([q.astype(BF16), kb16], axis=1), kb16)
        qk, kk = qkk[:, :c_len], qkk[:, c_len:]
        cols = col_ref[0, rows, :]
        col_lane = lax.broadcasted_iota(jnp.int32, cols.shape, 1)

        def pick(lane0):
            sel = jnp.where(col_lane == lane0 + head, cols, 0.0)
            return jnp.sum(sel, axis=-1, keepdims=True).reshape(grp, c_len, 1)

        for d in range(2):
            gc = pick(LANE_GC + GDN_HEADS * d)
            beta = pick(LANE_BETA + GDN_HEADS * d)
            gc_row = jnp.stack([row_ref[0, d, 0, pl.ds(gi * grp + g, 1), :] for g in range(grp)], axis=0)
            g_tot = gc[:, c_len - 1:c_len, :] if d == 0 else gc[:, 0:1, :]
            decay = jnp.exp(jnp.where(tri[d], gc - gc_row, -jnp.inf))
            kbeta = k * beta
            a = jnp.where(strict[d], kk * beta * decay, 0.0)
            t = eye - jnp.where(merge_masks[0], a, 0.0)
            for mm in merge_masks[1:]:
                t16 = t.astype(BF16)
                t = t - bmm(bmm(t16, jnp.where(mm, a, 0.0).astype(BF16)).astype(BF16), t16)
            rhs = jnp.concatenate([v * beta, kbeta * jnp.exp(gc)], axis=-1).astype(BF16)
            sol = bmm(t.astype(BF16), rhs)
            srows = pl.ds(pl.multiple_of(gi * (grp * HD), grp * HD), grp * HD)
            u_ref[d, rows, :] = sol[:, :, :HD].reshape(grp * c_len, HD)
            wq = jnp.concatenate([sol[:, :, HD:], q * jnp.exp(gc)], axis=1).astype(BF16)
            wq_ref[d, srows, :] = wq.reshape(grp * 2 * c_len, HD)
            att_ref[d, rows, :] = (qk * decay).reshape(grp * c_len, c_len).astype(BF16)
            kd = (k * jnp.exp(g_tot - gc)).astype(BF16)
            kd_uw = bmm_tn(kd, sol.astype(BF16)).astype(BF16)
            sq_ref[d, srows, :] = kd_uw[:, :, :HD].reshape(grp * HD, HD)
            sm_ref[d, srows, :] = kd_uw[:, :, HD:].reshape(grp * HD, HD)
            eg_ref[d, pl.ds(pl.multiple_of(gi * grp, grp), grp), :] = jnp.broadcast_to(
                jnp.exp(g_tot).reshape(grp, 1), (grp, HD))
        return 0

    lax.fori_loop(0, n_chunks // grp, local, 0)

    ctx_chunks = SUB // c_len

    def step(d, c, s):
        rows = pl.ds(pl.multiple_of(c * c_len, c_len), c_len)
        srows = pl.ds(pl.multiple_of(c * HD, HD), HD)
        s16 = s.astype(BF16)
        ws_qs = _dot(wq_ref[d, srows, :], s16)
        v_new = (u_ref[d, rows, :] - ws_qs[:c_len]).astype(BF16)
        acc_ref[d, rows, :] = ws_qs[c_len:] + _dot(att_ref[d, rows, :], v_new)
        return s * eg_ref[d, pl.ds(c, 1), :] + sq_ref[d, srows, :].astype(F32) - _dot(sm_ref[d, srows, :], s16)

    s0 = jnp.zeros((HD, HD), F32)
    scan_unroll = grp

    def scan(i, carry):
        s_f, s_b = carry
        for r in range(scan_unroll):
            j = i * scan_unroll + r
            s_f = step(0, j, s_f)
            s_b = step(1, jnp.where(j < ctx_chunks, ctx_chunks - 1 - j, n_chunks - 1 + ctx_chunks - j), s_b)
        return s_f, s_b

    lax.fori_loop(0, n_chunks // scan_unroll, scan, (s0, s0))

    def finish(s, _):
        rows = pl.ds(pl.multiple_of(s * SUB, SUB), SUB)
        o = acc_ref[0, rows, :] + acc_ref[1, rows, :]
        y = o * lax.rsqrt(jnp.mean(o * o, axis=-1, keepdims=True) + EPS) * nw_ref[...]
        o_ref[0, rows, :] = (_silu(g_ref[0, rows, :].astype(F32)) * y).astype(BF16)
        return 0

    lax.fori_loop(0, n_sub, finish, 0, unroll=True)


def _gdn(z3, conv_w, cols, rows, norm_w):
    b, t, _ = z3.shape
    cb = COL_GDN // HD
    n_chunks = t // GDN_CHUNK

    once = pl.Buffered(1)

    def col(off):
        return pl.BlockSpec((1, t, HD), lambda i, h: (i, 0, cb + off + h), pipeline_mode=once)

    def cw(off):
        return pl.BlockSpec((GDN_CONV, HD), lambda i, h: (0, off + h))

    return pl.pallas_call(
        functools.partial(_gdn_kernel, n_tok=t),
        grid=(b, GDN_HEADS),
        in_specs=[col(0), col(4), col(8), col(12), cw(0), cw(4), cw(8),
                  pl.BlockSpec((1, t, NZS), lambda i, h: (i, 0, 0), pipeline_mode=once),
                  pl.BlockSpec((1, 2, 1, n_chunks, GDN_CHUNK), lambda i, h: (i, 0, h, 0, 0)),
                  pl.BlockSpec((1, HD), lambda i, h: (0, 0))],
        out_specs=pl.BlockSpec((1, t, HD), lambda i, h: (i, 0, h)),
        out_shape=jax.ShapeDtypeStruct((b, t, GDN_HEADS * HD), BF16),
        scratch_shapes=[pltpu.VMEM((t + 24, HD), F32),
                        pltpu.VMEM((t, HD), F32), pltpu.VMEM((t, HD), F32), pltpu.VMEM((t, HD), F32),
                        pltpu.VMEM((2, 2 * t, HD), BF16), pltpu.VMEM((2, t, HD), F32),
                        pltpu.VMEM((2, n_chunks * HD, HD), BF16), pltpu.VMEM((2, n_chunks * HD, HD), BF16),
                        pltpu.VMEM((2, t, GDN_CHUNK), BF16),
                        pltpu.VMEM((2, n_chunks, HD), F32),
                        pltpu.VMEM((2, t, HD), F32)],
        compiler_params=_cparams(("arbitrary", "arbitrary")),
        name="gdn",
    )(z3, z3, z3, z3, conv_w, conv_w, conv_w, cols, rows, norm_w.reshape(1, HD))


def _mla_prep_kernel(cq_ref, ckv_ref, zs_ref, qnw_ref, kvnw_ref, wq_ref, wkv_ref, cos_ref, sin_ref,
                     q_ref, k_ref, v_ref):
    def rms(x, w):
        return (x * lax.rsqrt(jnp.mean(x * x, axis=-1, keepdims=True) + EPS) * w).astype(BF16)

    cos, sin = cos_ref[...], sin_ref[...]

    def rope(x):
        return x * cos + _swap_halves(x, 16) * sin

    q = _dot(rms(cq_ref[...].astype(F32), qnw_ref[...]), wq_ref[...])
    kv = _dot(rms(ckv_ref[...].astype(F32), kvnw_ref[...]), wkv_ref[...])
    lane = lax.broadcasted_iota(jnp.int32, (SUB, HD), 1)
    kr = rope(jnp.where(lane < MLA_ROPE, zs_ref[...], 0.0)).astype(BF16)
    for h in range(MLA_HEADS):
        base = h * MLA_QK
        q_ref[0, h, :, 0:HD] = (q[:, base:base + HD] * MLA_Q_SCALE).astype(BF16)
        q_ref[0, h, :, HD:MLA_QK] = (rope(q[:, base + HD:base + MLA_QK]) * MLA_Q_SCALE).astype(BF16)
        k_ref[0, h, :, 0:HD] = kv[:, h * HD:(h + 1) * HD].astype(BF16)
        k_ref[0, h, :, HD:MLA_QK] = kr
        v_ref[0, h] = kv[:, (MLA_HEADS + h) * HD:(MLA_HEADS + h + 1) * HD].astype(BF16)


def _mla_prep(z, zs, qnw, kvnw, wq, wkv, cos, sin, b, t):
    spb = t // SUB

    def tok(width, blk):
        return pl.BlockSpec((SUB, width), lambda i: (i, blk))

    def par(shape):
        return pl.BlockSpec(shape, lambda i: (0, 0))

    def head_out(width):
        return pl.BlockSpec((1, MLA_HEADS, SUB, width), lambda i: (i // spb, 0, i % spb, 0))

    return pl.pallas_call(
        _mla_prep_kernel,
        grid=(b * spb,),
        in_specs=[tok(MLA_Q_RANK, COL_CQ // MLA_Q_RANK), tok(MLA_KV_RANK, COL_CKV // MLA_KV_RANK), tok(NZS, 0),
                  par((1, MLA_Q_RANK)), par((1, MLA_KV_RANK)),
                  par(wq.shape), par(wkv.shape),
                  pl.BlockSpec((SUB, HD), lambda i: (i % spb, 0)),
                  pl.BlockSpec((SUB, HD), lambda i: (i % spb, 0))],
        out_specs=[head_out(MLA_QK), head_out(MLA_QK), head_out(HD)],
        out_shape=[jax.ShapeDtypeStruct((b, MLA_HEADS, t, MLA_QK), BF16),
                   jax.ShapeDtypeStruct((b, MLA_HEADS, t, MLA_QK), BF16),
                   jax.ShapeDtypeStruct((b, MLA_HEADS, t, HD), BF16)],
        compiler_params=_cparams(("arbitrary",)),
        name="mla_prep",
    )(z, z, zs, qnw.reshape(1, -1), kvnw.reshape(1, -1), wq, wkv, cos, sin)


MLA_HEADS_PER_STEP = 8


def _mla_attn_kernel(q_ref, k_ref, v_ref, *gate_and_out_refs):
    g_refs, o_ref = gate_and_out_refs[:-1], gate_and_out_refs[-1]

    def attend(n_keys):
        for hh in range(MLA_HEADS_PER_STEP):
            s = _dot_nt(q_ref[0, hh], k_ref[0, hh, 0:n_keys, :])
            p = jnp.exp2(s - jnp.max(s, axis=-1, keepdims=True))
            l = jnp.sum(p, axis=-1, keepdims=True)
            o = _dot(p.astype(BF16), v_ref[0, hh, 0:n_keys, :]) / l
            gate = _silu(g_refs[hh // 2][0, :, (hh % 2) * HD:(hh % 2 + 1) * HD].astype(F32))
            o_ref[0, :, hh * HD:(hh + 1) * HD] = (gate * o).astype(BF16)

    @pl.when(pl.program_id(2) == 0)
    def _():
        attend(SUB)

    @pl.when(pl.program_id(2) > 0)
    def _():
        attend(k_ref.shape[2])


def _mla_attn(qp, kp, vp, z3):
    b, nh, t, _ = qp.shape
    hs = MLA_HEADS_PER_STEP
    assert hs % 2 == 0 and nh % hs == 0 and COL_MGATE % (2 * HD) == 0
    gb = COL_MGATE // (2 * HD)
    once = pl.Buffered(1)

    def gate_spec(pair):
        return pl.BlockSpec((1, SUB, 2 * HD), lambda i, h, j: (i, j, gb + (hs // 2) * h + pair))

    return pl.pallas_call(
        _mla_attn_kernel,
        grid=(b, nh // hs, t // SUB),
        in_specs=[pl.BlockSpec((1, hs, SUB, MLA_QK), lambda i, h, j: (i, h, j, 0)),
                  pl.BlockSpec((1, hs, t, MLA_QK), lambda i, h, j: (i, h, 0, 0), pipeline_mode=once),
                  pl.BlockSpec((1, hs, t, HD), lambda i, h, j: (i, h, 0, 0), pipeline_mode=once)]
                 + [gate_spec(pair) for pair in range(hs // 2)],
        out_specs=pl.BlockSpec((1, SUB, hs * HD), lambda i, h, j: (i, j, h)),
        out_shape=jax.ShapeDtypeStruct((b, t, nh * HD), BF16),
        compiler_params=_cparams(("arbitrary", "arbitrary", "arbitrary")),
        name="mla_attn",
    )(qp, kp, vp, *([z3] * (hs // 2)))


def _layout_w_in(w_in):
    sizes = (512, 512, 512, 512, 512, 512, 512, 512, 8, 8, MLA_Q_RANK, MLA_KV_RANK, MLA_ROPE, 1024)
    offs = np.concatenate([[0], np.cumsum(sizes)])
    part = lambda n: w_in[:, offs[n]:offs[n + 1]]
    w_main = jnp.concatenate([part(n) for n in (0, 1, 2, 3, 4, 5, 6, 7, 11, 10, 13)], axis=-1)
    w_small = jnp.concatenate([part(12), part(8), part(9)], axis=-1)
    w_small = jnp.pad(w_small, ((0, 0), (0, NZS - w_small.shape[-1])))
    return w_main.astype(BF16), w_small.astype(BF16)


def _layout_w_qb(w_qb):
    w = w_qb.reshape(MLA_Q_RANK, MLA_HEADS, HD + MLA_ROPE)
    w = jnp.pad(w, ((0, 0), (0, 0), (0, MLA_QK - HD - MLA_ROPE)))
    return w.reshape(MLA_Q_RANK, MLA_HEADS * MLA_QK).astype(BF16)


def _layout_w_kvb(w_kvb):
    w = w_kvb.reshape(MLA_KV_RANK, MLA_HEADS, 2, HD)
    return jnp.transpose(w, (0, 2, 1, 3)).reshape(MLA_KV_RANK, 2 * MLA_HEADS * HD).astype(BF16)


def kernel(x, c, ctx, c_ctx, ada_w, ada_b, norm_w, w_in, ret_log_decay, ret_norm_w, gdn_conv_w, gdn_a_log,
           gdn_dt_bias, gdn_norm_w, mla_q_norm_w, mla_w_qb, mla_kv_norm_w, mla_w_kvb, w_out, final_norm_w):
    b, seq, d = x.shape
    n_ctx = ctx.shape[1]
    assert n_ctx == SUB and seq % SUB == 0 and d == D_MODEL
    depth = ada_w.shape[0]
    t = n_ctx + seq
    spb = t // SUB
    n_chunks = t // GDN_CHUNK

    cc = jnp.zeros((16, d), F32).at[:b].set(c).at[b].set(c_ctx)
    mod = _adaln(cc, ada_w, ada_b)
    sel = np.stack([np.full(b, b), np.arange(b)], axis=1).reshape(-1)

    ret_cos, ret_sin = _rope_tables(n_ctx, seq, HD)
    mla_cos, mla_sin = _rope_tables(n_ctx, seq, MLA_ROPE)

    xa = jnp.concatenate([ctx, x], axis=1).reshape(b * t, d)
    for i in range(depth):
        final = i == depth - 1
        shift, scale, gate = mod[i, :, :d], mod[i, :, d:2 * d], mod[i, :, 2 * d:]
        ms = jnp.stack([scale[sel], shift[sel]], axis=1)
        gate_tab = gate[sel][:, None, :]
        w_main, w_small = _layout_w_in(w_in[i])

        z, zs = _inproj(xa, ms, norm_w[i], w_main, w_small, spb)
        z3 = z.reshape(b, t, NZ)

        m_ret = _retention(z3, ret_log_decay[i], ret_norm_w[i], ret_cos, ret_sin)

        on_lanes = lambda p: jnp.zeros((1, NZS), F32).at[0, LANE_GC:LANE_GC + 2 * GDN_HEADS].set(p.reshape(-1))
        cols, rows_t = _gdn_gates(zs, on_lanes(gdn_a_log[i]), on_lanes(gdn_dt_bias[i]), b, t)
        gc_rows = rows_t[:, LANE_GC:LANE_GC + 2 * GDN_HEADS].reshape(b, 2, GDN_HEADS, n_chunks, GDN_CHUNK)
        m_gdn = _gdn(z3, gdn_conv_w[i], cols.reshape(b, t, NZS), gc_rows, gdn_norm_w[i])

        qp, kp, vp = _mla_prep(z, zs, mla_q_norm_w[i], mla_kv_norm_w[i], _layout_w_qb(mla_w_qb[i]),
                               _layout_w_kvb(mla_w_kvb[i]), mla_cos, mla_sin, b, t)
        m_mla = _mla_attn(qp, kp, vp, z3)

        xa = _outproj(xa, m_ret.reshape(b * t, -1), m_gdn.reshape(b * t, -1), m_mla.reshape(b * t, -1),
                      gate_tab, w_out[i].astype(BF16), final_norm_w, spb, final)
    return xa.reshape(b, seq, d)
```

```python
import functools
import math

import jax
import jax.numpy as jnp
import numpy as np
from jax import lax
from jax.experimental import pallas as pl
from jax.experimental.pallas import tpu as pltpu

F32 = jnp.float32
BF16 = jnp.bfloat16

EPS = 1e-6
ROPE_BASE = 10000.0
GRID_W = 64

D_MODEL = 2048
SUB = 256
RET_HEADS = 4
GDN_HEADS = 4
HD = 128
GDN_CONV = 5
GDN_CHUNK = 64
MLA_HEADS = 8
MLA_ROPE = 64
MLA_Q_RANK = 768
MLA_KV_RANK = 512
MLA_QK = 256
MLA_KW = 192
MLA_SCALE = (HD + MLA_ROPE) ** -0.5
MLA_Q_SCALE = MLA_SCALE * math.log2(math.e)

COL_RET = 0
COL_GDN = 2048
COL_CKV = 4096
COL_CQ = 4608
COL_MGATE = 5376
NZ = 6400
NZS = 128

VMEM_LIMIT = 56 * 1024 * 1024


def _cparams(sem, vmem_limit=VMEM_LIMIT):
    return pltpu.CompilerParams(dimension_semantics=sem, vmem_limit_bytes=vmem_limit)


def _silu(x):
    return x * (1.0 / (1.0 + jnp.exp(-x)))


def _dot(a, b):
    return jnp.dot(a, b, preferred_element_type=F32)


def _dot_nt(a, b):
    return lax.dot_general(a, b, (((1,), (1,)), ((), ())), preferred_element_type=F32)


def _dot_tn(a, b):
    return lax.dot_general(a, b, (((0,), (0,)), ((), ())), preferred_element_type=F32)


def _adaln_kernel(c_ref, w_ref, b_ref, o_ref):
    s = _silu(c_ref[...]).astype(BF16)
    o_ref[0] = _dot(s, w_ref[0].astype(BF16)) + b_ref[0]


def _adaln(cc, ada_w, ada_b):
    depth, d, n3 = ada_w.shape
    tn = 768
    return pl.pallas_call(
        _adaln_kernel,
        grid=(depth, n3 // tn),
        in_specs=[pl.BlockSpec((16, d), lambda i, j: (0, 0)),
                  pl.BlockSpec((1, d, tn), lambda i, j: (i, 0, j)),
                  pl.BlockSpec((1, 1, tn), lambda i, j: (i, 0, j))],
        out_specs=pl.BlockSpec((1, 16, tn), lambda i, j: (i, 0, j)),
        out_shape=jax.ShapeDtypeStruct((depth, 16, n3), F32),
        compiler_params=_cparams(("arbitrary", "arbitrary")),
        name="adaln",
    )(cc, ada_w, ada_b.reshape(depth, 1, n3))


def _mod_row(sub_idx, subs_per_batch):
    b = sub_idx // subs_per_batch
    is_lat = (sub_idx % subs_per_batch != 0).astype(jnp.int32)
    return b * 2 + is_lat


def _inproj_kernel(x_ref, ms_ref, nw_ref, w_ref, ws_ref, z_ref, zs_ref, h_ref, *, n_sub, subs_per_batch):
    i = pl.program_id(0)

    @pl.when(pl.program_id(1) == 0)
    def _():
        for u in range(n_sub):
            m = ms_ref[_mod_row(i * n_sub + u, subs_per_batch)]
            x = x_ref[u * SUB:(u + 1) * SUB, :]
            y = x * lax.rsqrt(jnp.mean(x * x, axis=-1, keepdims=True) + EPS) * nw_ref[...]
            h_ref[u * SUB:(u + 1) * SUB, :] = (y * (1.0 + m[0:1]) + m[1:2]).astype(BF16)
        zs_ref[...] = _dot(h_ref[...], ws_ref[...])

    z_ref[...] = _dot(h_ref[...], w_ref[...]).astype(BF16)


def _inproj(xa, ms, norm_w, w_main, w_small, subs_per_batch):
    bt, d = xa.shape
    tm = 1024 if bt % 1024 == 0 else SUB
    tn = 1280
    return pl.pallas_call(
        functools.partial(_inproj_kernel, n_sub=tm // SUB, subs_per_batch=subs_per_batch),
        grid=(bt // tm, NZ // tn),
        in_specs=[pl.BlockSpec((tm, d), lambda i, j: (i, 0)),
                  pl.BlockSpec(ms.shape, lambda i, j: (0, 0, 0)),
                  pl.BlockSpec((1, d), lambda i, j: (0, 0)),
                  pl.BlockSpec((d, tn), lambda i, j: (0, j)),
                  pl.BlockSpec((d, NZS), lambda i, j: (0, 0))],
        out_specs=[pl.BlockSpec((tm, tn), lambda i, j: (i, j)),
                   pl.BlockSpec((tm, NZS), lambda i, j: (i, 0))],
        out_shape=[jax.ShapeDtypeStruct((bt, NZ), BF16),
                   jax.ShapeDtypeStruct((bt, NZS), F32)],
        scratch_shapes=[pltpu.VMEM((tm, d), BF16)],
        compiler_params=_cparams(("arbitrary", "arbitrary")),
        name="inproj",
    )(xa, ms, norm_w.reshape(1, d), w_main, w_small)


def _outproj_kernel(x_ref, r_ref, g_ref, m_ref, gate_ref, w_ref, fw_ref, o_ref, *, n_sub, subs_per_batch, final):
    i = pl.program_id(0)
    acc = (_dot(r_ref[...], w_ref[0:512, :]) + _dot(g_ref[...], w_ref[512:1024, :])
           + _dot(m_ref[...], w_ref[1024:2048, :]))
    for u in range(n_sub):
        row = (i // (subs_per_batch - 1)) * 2 + 1 if final else _mod_row(i * n_sub + u, subs_per_batch)
        gate = gate_ref[row]
        rows = slice(u * SUB, (u + 1) * SUB)
        xn = x_ref[rows, :] + gate * acc[rows, :]
        if final:
            xn = xn * lax.rsqrt(jnp.mean(xn * xn, axis=-1, keepdims=True) + EPS) * fw_ref[...]
        o_ref[rows, :] = xn


def _outproj(xa, mret, mgdn, mmla, gate_tab, w_out, final_w, subs_per_batch, final):
    bt, d = xa.shape
    if final:
        tm, lat = SUB, subs_per_batch - 1
        n_blocks = (bt // SUB // subs_per_batch) * lat
        src = lambda i: ((i // lat) * subs_per_batch + 1 + i % lat, 0)
        aliases = {}
    else:
        tm = 512 if bt % 512 == 0 else SUB
        n_blocks = bt // tm
        src = lambda i: (i, 0)
        aliases = {0: 0}
    return pl.pallas_call(
        functools.partial(_outproj_kernel, n_sub=tm // SUB, subs_per_batch=subs_per_batch, final=final),
        grid=(n_blocks,),
        in_specs=[pl.BlockSpec((tm, d), src),
                  pl.BlockSpec((tm, 512), src),
                  pl.BlockSpec((tm, 512), src),
                  pl.BlockSpec((tm, 1024), src),
                  pl.BlockSpec(gate_tab.shape, lambda i: (0, 0, 0)),
                  pl.BlockSpec((d, d), lambda i: (0, 0)),
                  pl.BlockSpec((1, d), lambda i: (0, 0))],
        out_specs=pl.BlockSpec((tm, d), lambda i: (i, 0)),
        out_shape=jax.ShapeDtypeStruct((n_blocks * tm, d), F32),
        input_output_aliases=aliases,
        compiler_params=_cparams(("arbitrary",)),
        name="outproj",
    )(xa, mret, mgdn, mmla, gate_tab, w_out, final_w.reshape(1, d))


def _swap_halves(x, half):
    lane = lax.broadcasted_iota(jnp.int32, x.shape, x.ndim - 1)
    first = (lane % (2 * half)) < half
    return jnp.where(first, pltpu.roll(x, 128 - half, x.ndim - 1), pltpu.roll(x, half, x.ndim - 1))


def _rope_tables(n_ctx, seq, dim):
    rows = seq // GRID_W
    row = jnp.repeat(jnp.arange(rows), GRID_W).astype(F32)
    col = jnp.tile(jnp.arange(GRID_W), rows).astype(F32)
    m = dim // 2
    inv = ROPE_BASE ** (-jnp.arange(m // 2, dtype=F32) / (m // 2))
    ar, ac = row[:, None] * inv, col[:, None] * inv
    cos = jnp.concatenate([jnp.cos(ar), jnp.cos(ar), jnp.cos(ac), jnp.cos(ac)], axis=-1)
    sin = jnp.concatenate([-jnp.sin(ar), jnp.sin(ar), -jnp.sin(ac), jnp.sin(ac)], axis=-1)
    cos = jnp.pad(cos, ((n_ctx, 0), (0, 128 - dim)), constant_values=1.0)
    sin = jnp.pad(sin, ((n_ctx, 0), (0, 128 - dim)))
    return cos, sin


def _ret_kernel(dec_ref, q_ref, k_ref, v_ref, g_ref, cos_ref, sin_ref, nw_ref, o_ref,
                acc_ref, qd_ref, inc_ref, st_ref, *, n_chunks):
    h = pl.program_id(1)
    gf = dec_ref[0, h]
    gb = dec_ref[1, h]
    c_len = SUB

    ri = lax.broadcasted_iota(jnp.int32, (c_len, c_len), 0)
    ci = lax.broadcasted_iota(jnp.int32, (c_len, c_len), 1)
    dist = (ri - ci).astype(F32)
    mask = (jnp.where(ri >= ci, jnp.exp(gf * jnp.maximum(dist, 0.0)), 0.0)
            + jnp.where(ri <= ci, jnp.exp(gb * jnp.maximum(-dist, 0.0)), 0.0))
    pos = lax.broadcasted_iota(jnp.int32, (c_len, HD), 0).astype(F32)
    q_dec_f = jnp.exp(gf * (pos + 1.0))
    q_dec_b = jnp.exp(gb * (c_len - pos))
    k_dec_f = jnp.exp(gf * (c_len - 1.0 - pos))
    k_dec_b = jnp.exp(gb * pos)
    s_dec_f = jnp.exp(gf * c_len)
    s_dec_b = jnp.exp(gb * c_len)

    def rows_of(c, n=c_len):
        return pl.ds(c * n if isinstance(c, int) else pl.multiple_of(c * n, n), n)

    grp = math.gcd(n_chunks - 1, 16)

    def for_chunks(fn):
        fn(0)

        def body(g, _):
            for r in range(grp):
                fn(1 + g * grp + r)
            return 0

        lax.fori_loop(0, (n_chunks - 1) // grp, body, 0)

    def local(c):
        rows = rows_of(c)
        cos, sin = cos_ref[rows, :], sin_ref[rows, :]
        q = q_ref[0, rows, :].astype(F32)
        k = k_ref[0, rows, :].astype(F32)
        q = q * cos + _swap_halves(q, 32) * sin
        k = (k * cos + _swap_halves(k, 32) * sin) * (HD ** -0.5)
        v = v_ref[0, rows, :]
        p = (_dot_nt(q.astype(BF16), k.astype(BF16)) * mask).astype(BF16)
        acc_ref[rows, :] = _dot(p, v)
        qd_ref[0, rows, :] = (q * q_dec_f).astype(BF16)
        qd_ref[1, rows, :] = (q * q_dec_b).astype(BF16)
        inc_ref[0, rows_of(c, HD), :] = _dot_tn((k * k_dec_f).astype(BF16), v)
        inc_ref[1, rows_of(c, HD), :] = _dot_tn((k * k_dec_b).astype(BF16), v)

    for_chunks(local)

    def scan(i, carry):
        s_f, s_b = carry
        cb = jnp.where(i == 0, 0, n_chunks - i)
        st_ref[0, rows_of(i, HD), :] = s_f.astype(BF16)
        st_ref[1, rows_of(cb, HD), :] = s_b.astype(BF16)
        return (s_f * s_dec_f + inc_ref[0, rows_of(i, HD), :], s_b * s_dec_b + inc_ref[1, rows_of(cb, HD), :])

    lax.fori_loop(0, n_chunks, scan, (jnp.zeros((HD, HD), F32), jnp.zeros((HD, HD), F32)))

    def finish(c):
        rows = rows_of(c)
        o = (acc_ref[rows, :] + _dot(qd_ref[0, rows, :], st_ref[0, rows_of(c, HD), :])
             + _dot(qd_ref[1, rows, :], st_ref[1, rows_of(c, HD), :]))
        y = o * lax.rsqrt(jnp.mean(o * o, axis=-1, keepdims=True) + EPS) * nw_ref[...]
        o_ref[0, rows, :] = (_silu(g_ref[0, rows, :].astype(F32)) * y).astype(BF16)

    for_chunks(finish)


def _retention(z3, log_decay, norm_w, cos, sin):
    b, t, _ = z3.shape
    cb = COL_RET // HD

    def col(off):
        return pl.BlockSpec((1, t, HD), lambda i, h: (i, 0, cb + off + h))

    return pl.pallas_call(
        functools.partial(_ret_kernel, n_chunks=t // SUB),
        grid=(b, RET_HEADS),
        in_specs=[pl.BlockSpec(memory_space=pltpu.SMEM),
                  col(0), col(4), col(8), col(12),
                  pl.BlockSpec((t, HD), lambda i, h: (0, 0)),
                  pl.BlockSpec((t, HD), lambda i, h: (0, 0)),
                  pl.BlockSpec((1, HD), lambda i, h: (0, 0))],
        out_specs=pl.BlockSpec((1, t, HD), lambda i, h: (i, 0, h)),
        out_shape=jax.ShapeDtypeStruct((b, t, RET_HEADS * HD), BF16),
        scratch_shapes=[pltpu.VMEM((t, HD), F32), pltpu.VMEM((2, t, HD), BF16),
                        pltpu.VMEM((2, (t // SUB) * HD, HD), F32), pltpu.VMEM((2, (t // SUB) * HD, HD), BF16)],
        compiler_params=_cparams(("arbitrary", "arbitrary")),
        name="retention",
    )(log_decay, z3, z3, z3, z3, cos, sin, norm_w.reshape(1, HD))


LANE_GC = 64
LANE_BETA = 72


def _gdn_gate_kernel(zs_ref, alog_ref, dtb_ref, col_ref, row_ref):
    c = GDN_CHUNK
    pos = lax.broadcasted_iota(jnp.int32, (SUB, NZS), 0) % c
    lane = lax.broadcasted_iota(jnp.int32, (SUB, NZS), 1)
    for s in range(zs_ref.shape[0] // SUB):
        rows = slice(s * SUB, (s + 1) * SUB)
        x = zs_ref[rows, :]
        z = x + dtb_ref[...]
        softplus = jnp.maximum(z, 0.0) + jnp.log(1.0 + jnp.exp(-jnp.abs(z)))
        g = -jnp.exp(alog_ref[...]) * softplus
        pre = suf = g
        sh = 1
        while sh < c:
            pre = pre + jnp.where(pos >= sh, pltpu.roll(pre, sh, 0), 0.0)
            suf = suf + jnp.where(pos < c - sh, pltpu.roll(suf, SUB - sh, 0), 0.0)
            sh *= 2
        beta = 1.0 / (1.0 + jnp.exp(-x))
        cols = jnp.where(lane < LANE_GC + GDN_HEADS, pre, jnp.where(lane < LANE_BETA, suf, beta))
        col_ref[rows, :] = cols
        row_ref[0, :, rows] = cols.T


def _gdn_gates(zs, alog_lane, dtb_lane, b, t):
    return pl.pallas_call(
        _gdn_gate_kernel,
        grid=(b,),
        in_specs=[pl.BlockSpec((t, NZS), lambda i: (i, 0)),
                  pl.BlockSpec((1, NZS), lambda i: (0, 0)),
                  pl.BlockSpec((1, NZS), lambda i: (0, 0))],
        out_specs=[pl.BlockSpec((t, NZS), lambda i: (i, 0)),
                   pl.BlockSpec((1, NZS, t), lambda i: (i, 0, 0))],
        out_shape=[jax.ShapeDtypeStruct((b * t, NZS), F32),
                   jax.ShapeDtypeStruct((b, NZS, t), F32)],
        compiler_params=_cparams(("arbitrary",)),
        name="gdn_gates",
    )(zs, alog_lane, dtb_lane)


assert 2 * GDN_CHUNK == HD
GDN_GROUP = 34


def _gdn_kernel(q_ref, k_ref, v_ref, g_ref, cwq_ref, cwk_ref, cwv_ref, col_ref, row_ref, nw_ref, o_ref,
                pad_ref, q32_ref, k32_ref, v32_ref, wq_ref, u_ref, sq_ref, sm_ref, att_ref, eg_ref, acc_ref,
                *, n_tok):
    head = pl.program_id(1)
    c_len = GDN_CHUNK
    n_chunks = n_tok // c_len
    n_sub = n_tok // SUB
    grp = GDN_GROUP if n_chunks % GDN_GROUP == 0 else SUB // c_len
    halo = 8

    def pad_row(s):
        return pl.multiple_of(halo + s * SUB + halo * jnp.minimum(s, 1), 8)

    for gap in (0, halo + SUB, 2 * halo + n_tok):
        pad_ref[gap:gap + halo, :] = jnp.zeros((halo, HD), F32)

    for src_ref, cw_ref, dst_ref, norm_scale in ((q_ref, cwq_ref, q32_ref, HD ** -0.5),
                                                 (k_ref, cwk_ref, k32_ref, 1.0),
                                                 (v_ref, cwv_ref, v32_ref, None)):
        def fill(s, _):
            rows = pl.ds(pl.multiple_of(s * SUB, SUB), SUB)
            pad_ref[pl.ds(pad_row(s), SUB), :] = src_ref[0, rows, :].astype(F32)
            return 0

        lax.fori_loop(0, n_sub, fill, 0)
        cw = cw_ref[...]

        def conv(s, _):
            base = pl.multiple_of(pad_row(s) - halo, 8)
            y = None
            for j in range(GDN_CONV):
                tap = pad_ref[pl.ds(base + (halo + j - GDN_CONV // 2), SUB), :] * cw[j:j + 1, :]
                y = tap if y is None else y + tap
            y = _silu(y)
            if norm_scale is not None:
                y = y * (lax.rsqrt(jnp.sum(y * y, axis=-1, keepdims=True) + EPS) * norm_scale)
            dst_ref[pl.ds(pl.multiple_of(s * SUB, SUB), SUB), :] = y
            return 0

        lax.fori_loop(0, n_sub, conv, 0, unroll=True)

    ri = lax.broadcasted_iota(jnp.int32, (grp, c_len, c_len), 1)
    ci = lax.broadcasted_iota(jnp.int32, (grp, c_len, c_len), 2)
    eye = (ri == ci).astype(F32)
    tri = (ri >= ci, ri <= ci)
    strict = (ri > ci, ri < ci)
    merge_masks = []
    m = 1
    while m < c_len:
        merge_masks.append(((ri // (2 * m)) == (ci // (2 * m))) & ((ri // m) != (ci // m)))
        m *= 2

    def bmm(a, b):
        return lax.dot_general(a, b, (((2,), (1,)), ((0,), (0,))), preferred_element_type=F32)

    def bmm_nt(a, b):
        return lax.dot_general(a, b, (((2,), (2,)), ((0,), (0,))), preferred_element_type=F32)

    def bmm_tn(a, b):
        return lax.dot_general(a, b, (((1,), (1,)), ((0,), (0,))), preferred_element_type=F32)

    def local(gi, _):
        r0 = pl.multiple_of(gi * (grp * c_len), grp * c_len)
        rows = pl.ds(r0, grp * c_len)
        q = q32_ref[rows, :].reshape(grp, c_len, HD)
        k = k32_ref[rows, :].reshape(grp, c_len, HD)
        v = v32_ref[rows, :].reshape(grp, c_len, HD)
        kb16 = k.astype(BF16)
        qkk = bmm_nt(jnp.concatenate([q.astype(BF16), kb16], axis=1), kb16)
        qk, kk = qkk[:, :c_len], qkk[:, c_len:]
        cols = col_ref[0, rows, :]
        col_lane = lax.broadcasted_iota(jnp.int32, cols.shape, 1)

        def pick(lane0):
            sel = jnp.where(col_lane == lane0 + head, cols, 0.0)
            return jnp.sum(sel, axis=-1, keepdims=True).reshape(grp, c_len, 1)

        for d in range(2):
            gc = pick(LANE_GC + GDN_HEADS * d)
            beta = pick(LANE_BETA + GDN_HEADS * d)
            gc_row = jnp.stack([row_ref[0, d, 0, pl.ds(gi * grp + g, 1), :] for g in range(grp)], axis=0)
            g_tot = gc[:, c_len - 1:c_len, :] if d == 0 else gc[:, 0:1, :]
            decay = jnp.exp(jnp.where(tri[d], gc - gc_row, -jnp.inf))
            kbeta = k * beta
            a = jnp.where(strict[d], kk * beta * decay, 0.0)
            t = eye - jnp.where(merge_masks[0], a, 0.0)
            for mm in merge_masks[1:]:
                t16 = t.astype(BF16)
                t = t - bmm(bmm(t16, jnp.where(mm, a, 0.0).astype(BF16)).astype(BF16), t16)
            rhs = jnp.concatenate([v * beta, kbeta * jnp.exp(gc)], axis=-1).astype(BF16)
            sol = bmm(t.astype(BF16), rhs)
            srows = pl.ds(pl.multiple_of(gi * (grp * HD), grp * HD), grp * HD)
            u_ref[d, rows, :] = sol[:, :, :HD].reshape(grp * c_len, HD)
            wq = jnp.concatenate([sol[:, :, HD:], q * jnp.exp(gc)], axis=1).astype(BF16)
            wq_ref[d, srows, :] = wq.reshape(grp * 2 * c_len, HD)
            att_ref[d, rows, :] = (qk * decay).reshape(grp * c_len, c_len).astype(BF16)
            kd = (k * jnp.exp(g_tot - gc)).astype(BF16)
            kd_uw = bmm_tn(kd, sol.astype(BF16)).astype(BF16)
            sq_ref[d, srows, :] = kd_uw[:, :, :HD].reshape(grp * HD, HD)
            sm_ref[d, srows, :] = kd_uw[:, :, HD:].reshape(grp * HD, HD)
            eg_ref[d, pl.ds(pl.multiple_of(gi * grp, grp), grp), :] = jnp.broadcast_to(
                jnp.exp(g_tot).reshape(grp, 1), (grp, HD))
        return 0

    lax.fori_loop(0, n_chunks // grp, local, 0)

    ctx_chunks = SUB // c_len

    def step(d, c, s):
        rows = pl.ds(pl.multiple_of(c * c_len, c_len), c_len)
        srows = pl.ds(pl.multiple_of(c * HD, HD), HD)
        s16 = s.astype(BF16)
        ws_qs = _dot(wq_ref[d, srows, :], s16)
        v_new = (u_ref[d, rows, :] - ws_qs[:c_len]).astype(BF16)
        acc_ref[d, rows, :] = ws_qs[c_len:] + _dot(att_ref[d, rows, :], v_new)
        return s * eg_ref[d, pl.ds(c, 1), :] + sq_ref[d, srows, :].astype(F32) - _dot(sm_ref[d, srows, :], s16)

    s0 = jnp.zeros((HD, HD), F32)
    scan_unroll = grp

    def scan(i, carry):
        s_f, s_b = carry
        for r in range(scan_unroll):
            j = i * scan_unroll + r
            s_f = step(0, j, s_f)
            s_b = step(1, jnp.where(j < ctx_chunks, ctx_chunks - 1 - j, n_chunks - 1 + ctx_chunks - j), s_b)
        return s_f, s_b

    lax.fori_loop(0, n_chunks // scan_unroll, scan, (s0, s0))

    def finish(s, _):
        rows = pl.ds(pl.multiple_of(s * SUB, SUB), SUB)
        o = acc_ref[0, rows, :] + acc_ref[1, rows, :]
        y = o * lax.rsqrt(jnp.mean(o * o, axis=-1, keepdims=True) + EPS) * nw_ref[...]
        o_ref[0, rows, :] = (_silu(g_ref[0, rows, :].astype(F32)) * y).astype(BF16)
        return 0

    lax.fori_loop(0, n_sub, finish, 0, unroll=True)


def _gdn(z3, conv_w, cols, rows, norm_w):
    b, t, _ = z3.shape
    cb = COL_GDN // HD
    n_chunks = t // GDN_CHUNK

    once = pl.Buffered(1)

    def col(off):
        return pl.BlockSpec((1, t, HD), lambda i, h: (i, 0, cb + off + h), pipeline_mode=once)

    def cw(off):
        return pl.BlockSpec((GDN_CONV, HD), lambda i, h: (0, off + h))

    return pl.pallas_call(
        functools.partial(_gdn_kernel, n_tok=t),
        grid=(b, GDN_HEADS),
        in_specs=[col(0), col(4), col(8), col(12), cw(0), cw(4), cw(8),
                  pl.BlockSpec((1, t, NZS), lambda i, h: (i, 0, 0), pipeline_mode=once),
                  pl.BlockSpec((1, 2, 1, n_chunks, GDN_CHUNK), lambda i, h: (i, 0, h, 0, 0)),
                  pl.BlockSpec((1, HD), lambda i, h: (0, 0))],
        out_specs=pl.BlockSpec((1, t, HD), lambda i, h: (i, 0, h)),
        out_shape=jax.ShapeDtypeStruct((b, t, GDN_HEADS * HD), BF16),
        scratch_shapes=[pltpu.VMEM((t + 24, HD), F32),
                        pltpu.VMEM((t, HD), F32), pltpu.VMEM((t, HD), F32), pltpu.VMEM((t, HD), F32),
                        pltpu.VMEM((2, 2 * t, HD), BF16), pltpu.VMEM((2, t, HD), F32),
                        pltpu.VMEM((2, n_chunks * HD, HD), BF16), pltpu.VMEM((2, n_chunks * HD, HD), BF16),
                        pltpu.VMEM((2, t, GDN_CHUNK), BF16),
                        pltpu.VMEM((2, n_chunks, HD), F32),
                        pltpu.VMEM((2, t, HD), F32)],
        compiler_params=_cparams(("arbitrary", "arbitrary")),
        name="gdn",
    )(z3, z3, z3, z3, conv_w, conv_w, conv_w, cols, rows, norm_w.reshape(1, HD))


def _mla_prep_kernel(cq_ref, ckv_ref, zs_ref, qnw_ref, kvnw_ref, wq_ref, wkv_ref, cos_ref, sin_ref,
                     q_ref, k_ref, v_ref):
    def rms(x, w):
        return (x * lax.rsqrt(jnp.mean(x * x, axis=-1, keepdims=True) + EPS) * w).astype(BF16)

    cos, sin = cos_ref[...], sin_ref[...]

    def rope(x):
        return x * cos + _swap_halves(x, 16) * sin

    q = _dot(rms(cq_ref[...].astype(F32), qnw_ref[...]), wq_ref[...])
    kv = _dot(rms(ckv_ref[...].astype(F32), kvnw_ref[...]), wkv_ref[...])
    lane = lax.broadcasted_iota(jnp.int32, (SUB, HD), 1)
    kr = rope(jnp.where(lane < MLA_ROPE, zs_ref[...], 0.0)).astype(BF16)
    for h in range(MLA_HEADS):
        base = h * MLA_QK
        q_ref[0, h, :, 0:HD] = (q[:, base:base + HD] * MLA_Q_SCALE).astype(BF16)
        q_rope = (rope(q[:, base + HD:base + MLA_QK]) * MLA_Q_SCALE).astype(BF16)
        q_ref[0, h, :, HD:MLA_KW] = q_rope[:, 0:MLA_ROPE]
        k_ref[0, h, :, 0:HD] = kv[:, h * HD:(h + 1) * HD].astype(BF16)
        k_ref[0, h, :, HD:MLA_KW] = kr[:, 0:MLA_ROPE]
        v_ref[0, h] = kv[:, (MLA_HEADS + h) * HD:(MLA_HEADS + h + 1) * HD].astype(BF16)


def _mla_prep(z, zs, qnw, kvnw, wq, wkv, cos, sin, b, t):
    spb = t // SUB

    def tok(width, blk):
        return pl.BlockSpec((SUB, width), lambda i: (i, blk))

    def par(shape):
        return pl.BlockSpec(shape, lambda i: (0, 0))

    def head_out(width):
        return pl.BlockSpec((1, MLA_HEADS, SUB, width), lambda i: (i // spb, 0, i % spb, 0))

    return pl.pallas_call(
        _mla_prep_kernel,
        grid=(b * spb,),
        in_specs=[tok(MLA_Q_RANK, COL_CQ // MLA_Q_RANK), tok(MLA_KV_RANK, COL_CKV // MLA_KV_RANK), tok(NZS, 0),
                  par((1, MLA_Q_RANK)), par((1, MLA_KV_RANK)),
                  par(wq.shape), par(wkv.shape),
                  pl.BlockSpec((SUB, HD), lambda i: (i % spb, 0)),
                  pl.BlockSpec((SUB, HD), lambda i: (i % spb, 0))],
        out_specs=[head_out(MLA_KW), head_out(MLA_KW), head_out(HD)],
        out_shape=[jax.ShapeDtypeStruct((b, MLA_HEADS, t, MLA_KW), BF16),
                   jax.ShapeDtypeStruct((b, MLA_HEADS, t, MLA_KW), BF16),
                   jax.ShapeDtypeStruct((b, MLA_HEADS, t, HD), BF16)],
        compiler_params=_cparams(("arbitrary",)),
        name="mla_prep",
    )(z, z, zs, qnw.reshape(1, -1), kvnw.reshape(1, -1), wq, wkv, cos, sin)


MLA_HEADS_PER_STEP = 8


def _mla_attn_kernel(q_ref, k_ref, v_ref, *gate_and_out_refs):
    g_refs, o_ref = gate_and_out_refs[:-1], gate_and_out_refs[-1]

    def attend(n_keys):
        for hh in range(MLA_HEADS_PER_STEP):
            s = _dot_nt(q_ref[0, hh], k_ref[0, hh, 0:n_keys, :])
            p = jnp.exp2(s - jnp.max(s, axis=-1, keepdims=True))
            l = jnp.sum(p, axis=-1, keepdims=True)
            o = _dot(p.astype(BF16), v_ref[0, hh, 0:n_keys, :]) / l
            gate = _silu(g_refs[hh // 2][0, :, (hh % 2) * HD:(hh % 2 + 1) * HD].astype(F32))
            o_ref[0, :, hh * HD:(hh + 1) * HD] = (gate * o).astype(BF16)

    @pl.when(pl.program_id(2) == 0)
    def _():
        attend(SUB)

    @pl.when(pl.program_id(2) > 0)
    def _():
        attend(k_ref.shape[2])


def _mla_attn(qp, kp, vp, z3):
    b, nh, t, _ = qp.shape
    hs = MLA_HEADS_PER_STEP
    assert hs % 2 == 0 and nh % hs == 0 and COL_MGATE % (2 * HD) == 0
    gb = COL_MGATE // (2 * HD)
    once = pl.Buffered(1)

    def gate_spec(pair):
        return pl.BlockSpec((1, SUB, 2 * HD), lambda i, h, j: (i, j, gb + (hs // 2) * h + pair))

    return pl.pallas_call(
        _mla_attn_kernel,
        grid=(b, nh // hs, t // SUB),
        in_specs=[pl.BlockSpec((1, hs, SUB, MLA_KW), lambda i, h, j: (i, h, j, 0)),
                  pl.BlockSpec((1, hs, t, MLA_KW), lambda i, h, j: (i, h, 0, 0)),
                  pl.BlockSpec((1, hs, t, HD), lambda i, h, j: (i, h, 0, 0), pipeline_mode=once)]
                 + [gate_spec(pair) for pair in range(hs // 2)],
        out_specs=pl.BlockSpec((1, SUB, hs * HD), lambda i, h, j: (i, j, h)),
        out_shape=jax.ShapeDtypeStruct((b, t, nh * HD), BF16),
        compiler_params=_cparams(("arbitrary", "arbitrary", "arbitrary"), 60 * 1024 * 1024),
        name="mla_attn",
    )(qp, kp, vp, *([z3] * (hs // 2)))


def _layout_w_in(w_in):
    sizes = (512, 512, 512, 512, 512, 512, 512, 512, 8, 8, MLA_Q_RANK, MLA_KV_RANK, MLA_ROPE, 1024)
    offs = np.concatenate([[0], np.cumsum(sizes)])
    part = lambda n: w_in[:, offs[n]:offs[n + 1]]
    w_main = jnp.concatenate([part(n) for n in (0, 1, 2, 3, 4, 5, 6, 7, 11, 10, 13)], axis=-1)
    w_small = jnp.concatenate([part(12), part(8), part(9)], axis=-1)
    w_small = jnp.pad(w_small, ((0, 0), (0, NZS - w_small.shape[-1])))
    return w_main.astype(BF16), w_small.astype(BF16)


def _layout_w_qb(w_qb):
    w = w_qb.reshape(MLA_Q_RANK, MLA_HEADS, HD + MLA_ROPE)
    w = jnp.pad(w, ((0, 0), (0, 0), (0, MLA_QK - HD - MLA_ROPE)))
    return w.reshape(MLA_Q_RANK, MLA_HEADS * MLA_QK).astype(BF16)


def _layout_w_kvb(w_kvb):
    w = w_kvb.reshape(MLA_KV_RANK, MLA_HEADS, 2, HD)
    return jnp.transpose(w, (0, 2, 1, 3)).reshape(MLA_KV_RANK, 2 * MLA_HEADS * HD).astype(BF16)


def kernel(x, c, ctx, c_ctx, ada_w, ada_b, norm_w, w_in, ret_log_decay, ret_norm_w, gdn_conv_w, gdn_a_log,
           gdn_dt_bias, gdn_norm_w, mla_q_norm_w, mla_w_qb, mla_kv_norm_w, mla_w_kvb, w_out, final_norm_w):
    b, seq, d = x.shape
    n_ctx = ctx.shape[1]
    assert n_ctx == SUB and seq % SUB == 0 and d == D_MODEL
    depth = ada_w.shape[0]
    t = n_ctx + seq
    spb = t // SUB
    n_chunks = t // GDN_CHUNK

    cc = jnp.zeros((16, d), F32).at[:b].set(c).at[b].set(c_ctx)
    mod = _adaln(cc, ada_w, ada_b)
    sel = np.stack([np.full(b, b), np.arange(b)], axis=1).reshape(-1)

    ret_cos, ret_sin = _rope_tables(n_ctx, seq, HD)
    mla_cos, mla_sin = _rope_tables(n_ctx, seq, MLA_ROPE)

    xa = jnp.concatenate([ctx, x], axis=1).reshape(b * t, d)
    for i in range(depth):
        final = i == depth - 1
        shift, scale, gate = mod[i, :, :d], mod[i, :, d:2 * d], mod[i, :, 2 * d:]
        ms = jnp.stack([scale[sel], shift[sel]], axis=1)
        gate_tab = gate[sel][:, None, :]
        w_main, w_small = _layout_w_in(w_in[i])

        z, zs = _inproj(xa, ms, norm_w[i], w_main, w_small, spb)
        z3 = z.reshape(b, t, NZ)

        m_ret = _retention(z3, ret_log_decay[i], ret_norm_w[i], ret_cos, ret_sin)

        on_lanes = lambda p: jnp.zeros((1, NZS), F32).at[0, LANE_GC:LANE_GC + 2 * GDN_HEADS].set(p.reshape(-1))
        cols, rows_t = _gdn_gates(zs, on_lanes(gdn_a_log[i]), on_lanes(gdn_dt_bias[i]), b, t)
        gc_rows = rows_t[:, LANE_GC:LANE_GC + 2 * GDN_HEADS].reshape(b, 2, GDN_HEADS, n_chunks, GDN_CHUNK)
        m_gdn = _gdn(z3, gdn_conv_w[i], cols.reshape(b, t, NZS), gc_rows, gdn_norm_w[i])

        qp, kp, vp = _mla_prep(z, zs, mla_q_norm_w[i], mla_kv_norm_w[i], _layout_w_qb(mla_w_qb[i]),
                               _layout_w_kvb(mla_w_kvb[i]), mla_cos, mla_sin, b, t)
        m_mla = _mla_attn(qp, kp, vp, z3)

        xa = _outproj(xa, m_ret.reshape(b * t, -1), m_gdn.reshape(b * t, -1), m_mla.reshape(b * t, -1),
                      gate_tab, w_out[i].astype(BF16), final_norm_w, spb, final)
    return xa.reshape(b, seq, d)
```
